```python
import math
import jax, jax.numpy as jnp
from jax import lax
import numpy as np

D_MODEL = 2048
BATCH = 4
SEQ = 8192
DEPTH = 1

CHUNK = 64
N_META = 16
Q_BLOCK = 128
MIX_WIDTH = D_MODEL
ATTN_WIDTH = MIX_WIDTH // 2
POOL_WIDTH = MIX_WIDTH - ATTN_WIDTH
N_ATTN_HEADS = 8
QK_HEAD_DIM = ATTN_WIDTH // (2 * N_ATTN_HEADS)
V_HEAD_DIM = 2 * QK_HEAD_DIM
POOL_WINDOWS = (2, 4, 8, 16)
N_POOL_GROUPS = len(POOL_WINDOWS)
POOL_GROUP_WIDTH = POOL_WIDTH // N_POOL_GROUPS
IN_PROJ_WIDTH = 3 * ATTN_WIDTH + POOL_WIDTH
D_FF = 5632
LN_EPS = 1e-5
DEEPNORM_ALPHA = (2.0 * DEPTH) ** 0.25
DEEPNORM_BETA = (8.0 * DEPTH) ** -0.25
NEG_INF = -1e30

kernel_name = 'hymba_diffattn_pool_macaron_deepnorm'


def layer_norm(x, g, b):
    xf = x.astype(jnp.float32)
    mu = jnp.mean(xf, axis=-1, keepdims=True)
    var = jnp.mean(jnp.square(xf - mu), axis=-1, keepdims=True)
    y = (xf - mu) * lax.rsqrt(var + LN_EPS)
    return (y * g.astype(jnp.float32) + b.astype(jnp.float32)).astype(x.dtype)


def swiglu(x, w_gate, w_up, w_down):
    return (jax.nn.silu(x @ w_gate) * (x @ w_up)) @ w_down


def chunk_ids(pos):
    return jnp.where(pos < N_META, 0, 1 + (pos - N_META) // CHUNK)


def alibi_slopes(n_heads):
    return 2.0 ** (-8.0 * jnp.arange(1, n_heads + 1, dtype=jnp.float32) / n_heads)


def diff_attention(q, k, v, lam, lam_init, subln_g):
    B, Lp, H, _, dk = q.shape
    dv = v.shape[-1]
    nb = Lp // Q_BLOCK
    pos = jnp.arange(Lp, dtype=jnp.int32)
    cid = chunk_ids(pos)
    slopes = alibi_slopes(H)
    scale = dk ** -0.5
    qb = q.reshape(B, nb, Q_BLOCK, H, 2, dk).transpose(1, 0, 3, 4, 2, 5)
    kt = k.transpose(0, 2, 3, 1, 4)
    vt = v.transpose(0, 2, 1, 3)
    pos_b = pos.reshape(nb, Q_BLOCK)

    def block(args):
        q_blk, qpos = args
        s = jnp.einsum('bhmqd,bhmkd->bhmqk', q_blk, kt,
                       preferred_element_type=jnp.float32) * scale
        dist = jnp.abs(qpos[:, None] - pos[None, :]).astype(jnp.float32)
        s = s - slopes[None, :, None, None, None] * dist[None, None, None]
        visible = chunk_ids(qpos)[:, None] >= cid[None, :]
        s = jnp.where(visible[None, None, None], s, NEG_INF)
        p = jax.nn.softmax(s, axis=-1)
        a = p[:, :, 0] - lam * p[:, :, 1]
        return jnp.einsum('bhqk,bhkd->bhqd', a.astype(vt.dtype), vt)

    o = lax.map(block, (qb, pos_b))
    o = o.transpose(1, 0, 3, 2, 4).reshape(B, Lp, H, dv).astype(jnp.float32)
    o = o * lax.rsqrt(jnp.mean(jnp.square(o), axis=-1, keepdims=True) + LN_EPS)
    o = o * subln_g.astype(jnp.float32) * (1.0 - lam_init)
    return o.astype(q.dtype).reshape(B, Lp, H * dv)


def multiscale_pool(u, w_pool, pool_scale):
    B, Lp, _ = u.shape
    ug = u.reshape(B, Lp, N_POOL_GROUPS, POOL_GROUP_WIDTH)
    cs = jnp.cumsum(ug.astype(jnp.float32), axis=1)
    t1 = jnp.arange(1, Lp + 1, dtype=jnp.float32)
    pooled = []
    for g, w in enumerate(POOL_WINDOWS):
        csg = cs[:, :, g]
        prev = jnp.pad(csg, ((0, 0), (w, 0), (0, 0)))[:, :Lp]
        count = jnp.minimum(t1, float(w))
        mean = (csg - prev) / count[None, :, None]
        pooled.append(mean - ug[:, :, g].astype(jnp.float32))
    pooled = jnp.stack(pooled, axis=2).astype(u.dtype)
    y = jnp.einsum('blgc,gcd->blgd', pooled, w_pool)
    return y.reshape(B, Lp, POOL_WIDTH) * pool_scale


def setup_inputs(seed: int = 0) -> dict:
    key = jax.random.key(seed)
    ks = jax.random.split(key, 24)
    f32 = jnp.float32
    nrm = lambda k, shape, s: jax.random.normal(k, shape, f32) * s
    return {
        'x': nrm(ks[0], (BATCH, SEQ, D_MODEL), 1.0),
        'meta_tokens': nrm(ks[1], (N_META, D_MODEL), 1.0),
        'ln1_g': 1.0 + nrm(ks[2], (DEPTH, D_MODEL), 0.02),
        'ln1_b': nrm(ks[3], (DEPTH, D_MODEL), 0.02),
        'ffn1_w_gate': nrm(ks[4], (DEPTH, D_MODEL, D_FF), D_MODEL ** -0.5),
        'ffn1_w_up': nrm(ks[5], (DEPTH, D_MODEL, D_FF), D_MODEL ** -0.5),
        'ffn1_w_down': nrm(ks[6], (DEPTH, D_FF, D_MODEL), DEEPNORM_BETA * D_FF ** -0.5),
        'w_in': nrm(ks[7], (DEPTH, D_MODEL, IN_PROJ_WIDTH), D_MODEL ** -0.5),
        'lambda_q1': nrm(ks[8], (DEPTH, QK_HEAD_DIM), 0.1),
        'lambda_k1': nrm(ks[9], (DEPTH, QK_HEAD_DIM), 0.1),
        'lambda_q2': nrm(ks[10], (DEPTH, QK_HEAD_DIM), 0.1),
        'lambda_k2': nrm(ks[11], (DEPTH, QK_HEAD_DIM), 0.1),
        'subln_g': 1.0 + nrm(ks[12], (DEPTH, V_HEAD_DIM), 0.02),
        'w_pool': nrm(ks[13], (DEPTH, N_POOL_GROUPS, POOL_GROUP_WIDTH, POOL_GROUP_WIDTH), POOL_GROUP_WIDTH ** -0.5),
        'pool_scale': 1.0 + nrm(ks[14], (DEPTH, POOL_WIDTH), 0.1),
        'w_out': nrm(ks[15], (DEPTH, MIX_WIDTH, D_MODEL), DEEPNORM_BETA * MIX_WIDTH ** -0.5),
        'ln2_g': 1.0 + nrm(ks[16], (DEPTH, D_MODEL), 0.02),
        'ln2_b': nrm(ks[17], (DEPTH, D_MODEL), 0.02),
        'ffn2_w_gate': nrm(ks[18], (DEPTH, D_MODEL, D_FF), D_MODEL ** -0.5),
        'ffn2_w_up': nrm(ks[19], (DEPTH, D_MODEL, D_FF), D_MODEL ** -0.5),
        'ffn2_w_down': nrm(ks[20], (DEPTH, D_FF, D_MODEL), DEEPNORM_BETA * D_FF ** -0.5),
        'ln3_g': 1.0 + nrm(ks[21], (DEPTH, D_MODEL), 0.02),
        'ln3_b': nrm(ks[22], (DEPTH, D_MODEL), 0.02),
    }


def reference(x, meta_tokens, ln1_g, ln1_b, ffn1_w_gate, ffn1_w_up, ffn1_w_down,
              w_in, lambda_q1, lambda_k1, lambda_q2, lambda_k2, subln_g, w_pool,
              pool_scale, w_out, ln2_g, ln2_b, ffn2_w_gate, ffn2_w_up, ffn2_w_down,
              ln3_g, ln3_b):
    B, S, D = x.shape
    L = S + N_META
    Lp = ((L + Q_BLOCK - 1) // Q_BLOCK) * Q_BLOCK
    meta = jnp.broadcast_to(meta_tokens.astype(x.dtype)[None], (B, N_META, D))
    pad = jnp.zeros((B, Lp - L, D), x.dtype)
    h = jnp.concatenate([meta, x, pad], axis=1)
    H, dk = N_ATTN_HEADS, QK_HEAD_DIM
    for l in range(DEPTH):
        h = layer_norm(DEEPNORM_ALPHA * h + 0.5 * swiglu(h, ffn1_w_gate[l], ffn1_w_up[l], ffn1_w_down[l]),
                       ln1_g[l], ln1_b[l])
        xw = h @ w_in[l]
        q = xw[..., :ATTN_WIDTH].reshape(B, Lp, H, 2, dk)
        k = xw[..., ATTN_WIDTH:2 * ATTN_WIDTH].reshape(B, Lp, H, 2, dk)
        v = xw[..., 2 * ATTN_WIDTH:3 * ATTN_WIDTH].reshape(B, Lp, H, V_HEAD_DIM)
        u = xw[..., 3 * ATTN_WIDTH:]
        lam_init = 0.8 - 0.6 * math.exp(-0.3 * l)
        lam = (jnp.exp(jnp.sum(lambda_q1[l].astype(jnp.float32) * lambda_k1[l].astype(jnp.float32)))
               - jnp.exp(jnp.sum(lambda_q2[l].astype(jnp.float32) * lambda_k2[l].astype(jnp.float32)))
               + lam_init)
        a_out = diff_attention(q, k, v, lam, lam_init, subln_g[l])
        p_out = multiscale_pool(u, w_pool[l], pool_scale[l])
        mix = jnp.concatenate([a_out, p_out.astype(a_out.dtype)], axis=-1) @ w_out[l]
        h = layer_norm(DEEPNORM_ALPHA * h + mix, ln2_g[l], ln2_b[l])
        h = layer_norm(DEEPNORM_ALPHA * h + 0.5 * swiglu(h, ffn2_w_gate[l], ffn2_w_up[l], ffn2_w_down[l]),
                       ln3_g[l], ln3_b[l])
    return h[:, N_META:N_META + S]
```

```python
import functools
import math

import jax
import jax.numpy as jnp
from jax import lax
from jax.experimental import pallas as pl
from jax.experimental.pallas import tpu as pltpu

F32 = jnp.float32
BF16 = jnp.bfloat16

CHUNK = 64
N_META = 16
N_HEADS = 8
QK_DIM = 64
V_DIM = 128
POOL_WINDOWS = (2, 4, 8, 16)
POOL_GROUP = 256
LN_EPS = 1e-5
DEPTH = 1
ALPHA = (2.0 * DEPTH) ** 0.25
LAM_INIT = 0.8 - 0.6 * math.exp(-0.3 * 0)
NEG_INF = -1e30

VMEM_LIMIT = 56 * 1024 * 1024


def _layer_norm(y, g, b):
    mu = jnp.mean(y, axis=-1, keepdims=True)
    yc = y - mu
    var = jnp.mean(yc * yc, axis=-1, keepdims=True)
    return yc * lax.rsqrt(var + LN_EPS) * g + b


def _ffn_ln_kernel(h_ref, wg_ref, wu_ref, wd_ref, g_ref, b_ref, o_ref, ob_ref, hb_ref, acc_ref):
    f = pl.program_id(1)
    nf = pl.num_programs(1)

    @pl.when(f == 0)
    def _():
        hb_ref[...] = h_ref[...].astype(BF16)

    hb = hb_ref[...]
    gate = jnp.dot(hb, wg_ref[...], preferred_element_type=F32)
    up = jnp.dot(hb, wu_ref[...], preferred_element_type=F32)
    act = (gate / (1.0 + jnp.exp(-gate)) * up).astype(BF16)
    part = jnp.dot(act, wd_ref[...], preferred_element_type=F32)

    @pl.when(f == 0)
    def _():
        acc_ref[...] = part

    @pl.when(f > 0)
    def _():
        acc_ref[...] += part

    @pl.when(f == nf - 1)
    def _():
        y = ALPHA * h_ref[...] + 0.5 * acc_ref[...]
        out = _layer_norm(y, g_ref[...], b_ref[...])
        o_ref[...] = out
        ob_ref[...] = out.astype(BF16)


def _ffn_ln(h, wg, wu, wd, g, b, *, tm, tf):
    n, d = h.shape
    ff = wg.shape[1]
    grid = (n // tm, ff // tf)
    return pl.pallas_call(
        _ffn_ln_kernel,
        out_shape=(jax.ShapeDtypeStruct((n, d), F32), jax.ShapeDtypeStruct((n, d), BF16)),
        grid=grid,
        in_specs=[
            pl.BlockSpec((tm, d), lambda i, f: (i, 0)),
            pl.BlockSpec((d, tf), lambda i, f: (0, f)),
            pl.BlockSpec((d, tf), lambda i, f: (0, f)),
            pl.BlockSpec((tf, d), lambda i, f: (f, 0)),
            pl.BlockSpec((1, d), lambda i, f: (0, 0)),
            pl.BlockSpec((1, d), lambda i, f: (0, 0)),
        ],
        out_specs=(
            pl.BlockSpec((tm, d), lambda i, f: (i, 0)),
            pl.BlockSpec((tm, d), lambda i, f: (i, 0)),
        ),
        scratch_shapes=[pltpu.VMEM((tm, d), BF16), pltpu.VMEM((tm, d), F32)],
        compiler_params=pltpu.CompilerParams(
            dimension_semantics=("arbitrary", "arbitrary"), vmem_limit_bytes=VMEM_LIMIT),
        name="ffn_ln",
    )(h, wg, wu, wd, g, b)


def _in_proj_kernel(x_ref, w_ref, q_ref, k_ref, v_ref, u_ref):
    x = x_ref[...]
    aw = q_ref.shape[1]
    scale = QK_DIM ** -0.5
    q_ref[...] = (jnp.dot(x, w_ref[:, 0:aw], preferred_element_type=F32) * scale).astype(BF16)
    k_ref[...] = jnp.dot(x, w_ref[:, aw:2 * aw], preferred_element_type=F32).astype(BF16)
    v_ref[...] = jnp.dot(x, w_ref[:, 2 * aw:3 * aw], preferred_element_type=F32).astype(BF16)
    u_ref[...] = jnp.dot(x, w_ref[:, 3 * aw:], preferred_element_type=F32)


def _in_proj(x, w, *, tm):
    n, d = x.shape
    aw = N_HEADS * V_DIM
    pw = w.shape[1] - 3 * aw
    row = lambda i: (i, 0)
    return pl.pallas_call(
        _in_proj_kernel,
        out_shape=(
            jax.ShapeDtypeStruct((n, aw), BF16),
            jax.ShapeDtypeStruct((n, aw), BF16),
            jax.ShapeDtypeStruct((n, aw), BF16),
            jax.ShapeDtypeStruct((n, pw), F32),
        ),
        grid=(n // tm,),
        in_specs=[
            pl.BlockSpec((tm, d), row),
            pl.BlockSpec(w.shape, lambda i: (0, 0), pipeline_mode=pl.Buffered(1)),
        ],
        out_specs=(
            pl.BlockSpec((tm, aw), row),
            pl.BlockSpec((tm, aw), row),
            pl.BlockSpec((tm, aw), row),
            pl.BlockSpec((tm, pw), row),
        ),
        compiler_params=pltpu.CompilerParams(
            dimension_semantics=("arbitrary",), vmem_limit_bytes=VMEM_LIMIT),
        name="in_proj",
    )(x, w)


def _attn_kernel(q_ref, k_ref, v_ref, km_ref, vm_ref, slope_ref, lq1_ref, lk1_ref, lq2_ref, lk2_ref,
                 g_ref, o_ref, qs_ref, m_ref, l_ref, acc_ref, boff_ref, bdiag_ref, *, tq):
    qi = pl.program_id(2)
    rows = 2 * tq
    slope = slope_ref[0]
    slope1 = slope[:, 0:1]
    nt = (((1,), (1,)), ((), ()))

    @pl.when(qi == 0)
    def _():
        r = lax.broadcasted_iota(jnp.int32, (rows, tq), 0)
        r = jnp.where(r >= tq, r - tq, r)
        c = lax.broadcasted_iota(jnp.int32, (rows, tq), 1)
        rel = (r - c).astype(F32)
        boff_ref[...] = -slope1 * rel
        visible = (r // CHUNK) >= (c // CHUNK)
        bdiag_ref[...] = jnp.where(visible, -slope1 * jnp.abs(rel), NEG_INF)

    q = q_ref[0]
    lane = lax.broadcasted_iota(jnp.int32, q.shape, 1)
    zero = jnp.zeros_like(q)
    qs_ref[0:tq, :] = jnp.where(lane < QK_DIM, q, zero)
    qs_ref[tq:rows, :] = jnp.where(lane >= QK_DIM, q, zero)
    qs = qs_ref[...]

    s = lax.dot_general(qs, km_ref[...], nt, preferred_element_type=F32)
    r = lax.broadcasted_iota(jnp.int32, (rows, N_META), 0)
    r = jnp.where(r >= tq, r - tq, r)
    c = lax.broadcasted_iota(jnp.int32, (rows, N_META), 1)
    dist = (r - c + (N_META + qi * tq)).astype(F32)
    s = s - slope1 * dist
    m0 = jnp.max(s, axis=1, keepdims=True)
    p = jnp.exp(s - m0)
    m_ref[...] = jnp.broadcast_to(m0, (rows, V_DIM))
    l_ref[...] = jnp.broadcast_to(jnp.sum(p, axis=1, keepdims=True), (rows, V_DIM))
    acc_ref[...] = jnp.dot(p.astype(BF16), vm_ref[...], preferred_element_type=F32)

    def update(k_blk, v_blk, bias, shift):
        s = lax.dot_general(qs, k_blk, nt, preferred_element_type=F32) + bias
        m_prev = m_ref[...]
        m_cur = jnp.max(s, axis=1, keepdims=True) + shift
        m_new = jnp.maximum(m_prev, m_cur)
        p = jnp.exp(s - (m_new[:, 0:1] - shift))
        alpha = jnp.exp(m_prev - m_new)
        l_ref[...] = alpha * l_ref[...] + jnp.sum(p, axis=1, keepdims=True)
        acc_ref[...] = alpha * acc_ref[...] + jnp.dot(p.astype(BF16), v_blk, preferred_element_type=F32)
        m_ref[...] = m_new

    def body(ki, carry):
        start = pl.multiple_of(ki * tq, tq)
        k_blk = k_ref[0, pl.ds(start, tq), :]
        v_blk = v_ref[0, pl.ds(start, tq), :]
        gap = jnp.full((1, 1), (qi - ki) * tq, jnp.int32).astype(F32)
        update(k_blk, v_blk, boff_ref[...], -slope1 * gap)
        return carry

    lax.fori_loop(0, qi, body, 0)

    start = pl.multiple_of(qi * tq, tq)
    update(k_ref[0, pl.ds(start, tq), :], v_ref[0, pl.ds(start, tq), :], bdiag_ref[...],
           jnp.zeros((1, 1), F32))

    lam = (jnp.exp(jnp.sum(lq1_ref[...] * lk1_ref[...], axis=1, keepdims=True))
           - jnp.exp(jnp.sum(lq2_ref[...] * lk2_ref[...], axis=1, keepdims=True)) + LAM_INIT)
    o = acc_ref[...] / l_ref[...]
    d = o[0:tq] - lam * o[tq:rows]
    d = d * lax.rsqrt(jnp.mean(d * d, axis=-1, keepdims=True) + LN_EPS)
    o_ref[0] = (d * g_ref[...] * (1.0 - LAM_INIT)).astype(o_ref.dtype)


def _diff_attn(q, k, v, km, vm, slopes, lq1, lk1, lq2, lk2, g, *, tq):
    b, s, _ = q.shape
    rows = 2 * tq
    small = lambda shape: pl.BlockSpec(shape, lambda bi, h, i: (0,) * len(shape))
    return pl.pallas_call(
        functools.partial(_attn_kernel, tq=tq),
        out_shape=jax.ShapeDtypeStruct((b, s, N_HEADS * V_DIM), BF16),
        grid=(b, N_HEADS, s // tq),
        in_specs=[
            pl.BlockSpec((1, tq, V_DIM), lambda bi, h, i: (bi, i, h)),
            pl.BlockSpec((1, s, V_DIM), lambda bi, h, i: (bi, 0, h)),
            pl.BlockSpec((1, s, V_DIM), lambda bi, h, i: (bi, 0, h)),
            pl.BlockSpec((N_META, V_DIM), lambda bi, h, i: (0, h)),
            pl.BlockSpec((N_META, V_DIM), lambda bi, h, i: (0, h)),
            pl.BlockSpec((1, 1, V_DIM), lambda bi, h, i: (h, 0, 0)),
            small((1, QK_DIM)), small((1, QK_DIM)), small((1, QK_DIM)), small((1, QK_DIM)),
            small((1, V_DIM)),
        ],
        out_specs=pl.BlockSpec((1, tq, V_DIM), lambda bi, h, i: (bi, i, h)),
        scratch_shapes=[
            pltpu.VMEM((rows, V_DIM), BF16),
            pltpu.VMEM((rows, V_DIM), F32),
            pltpu.VMEM((rows, V_DIM), F32),
            pltpu.VMEM((rows, V_DIM), F32),
            pltpu.VMEM((rows, tq), F32),
            pltpu.VMEM((rows, tq), F32),
        ],
        compiler_params=pltpu.CompilerParams(
            dimension_semantics=("arbitrary", "arbitrary", "arbitrary"), vmem_limit_bytes=VMEM_LIMIT),
        name="diff_attn",
    )(q, k, v, km, vm, slopes, lq1, lk1, lq2, lk2, g)


def _mix_ln_kernel(a_ref, u_ref, halo_ref, um_ref, h_ref, wp_ref, ps_ref, wo_ref, g_ref, b_ref,
                   o_ref, ext_ref, p_ref, *, tm):
    i = pl.program_id(1)
    hw = N_META

    @pl.when(i == 0)
    def _():
        ext_ref[0:hw, :] = um_ref[...]

    @pl.when(i > 0)
    def _():
        ext_ref[0:hw, :] = halo_ref[0]

    ext_ref[hw:hw + tm, :] = u_ref[0]

    for gi, w in enumerate(POOL_WINDOWS):
        cols = slice(gi * POOL_GROUP, (gi + 1) * POOL_GROUP)
        cur = ext_ref[hw:hw + tm, cols]
        win = cur
        for back in range(1, w):
            win = win + ext_ref[hw - back:hw - back + tm, cols]
        pooled = win * (1.0 / w) - cur
        y = jnp.dot(pooled.astype(BF16), wp_ref[gi], preferred_element_type=F32)
        p_ref[:, cols] = (y * ps_ref[:, cols]).astype(BF16)

    aw = a_ref.shape[2]
    mix = jnp.dot(a_ref[0], wo_ref[0:aw, :], preferred_element_type=F32)
    mix = mix + jnp.dot(p_ref[...], wo_ref[aw:, :], preferred_element_type=F32)
    o_ref[0] = _layer_norm(ALPHA * h_ref[0] + mix, g_ref[...], b_ref[...])


def _mix_ln(a, u, um, h, wp, ps, wo, g, b, *, tm):
    bsz, s, d = h.shape
    aw = a.shape[2]
    pw = u.shape[2]
    hb = tm // N_META
    const2 = lambda shape: pl.BlockSpec(shape, lambda bi, i: (0,) * len(shape), pipeline_mode=pl.Buffered(1))
    return pl.pallas_call(
        functools.partial(_mix_ln_kernel, tm=tm),
        out_shape=jax.ShapeDtypeStruct((bsz, s, d), F32),
        grid=(bsz, s // tm),
        in_specs=[
            pl.BlockSpec((1, tm, aw), lambda bi, i: (bi, i, 0)),
            pl.BlockSpec((1, tm, pw), lambda bi, i: (bi, i, 0)),
            pl.BlockSpec((1, N_META, pw), lambda bi, i: (bi, jnp.maximum(i * hb - 1, 0), 0)),
            const2((N_META, pw)),
            pl.BlockSpec((1, tm, d), lambda bi, i: (bi, i, 0)),
            const2(wp.shape),
            const2((1, pw)),
            const2(wo.shape),
            const2((1, d)),
            const2((1, d)),
        ],
        out_specs=pl.BlockSpec((1, tm, d), lambda bi, i: (bi, i, 0)),
        scratch_shapes=[pltpu.VMEM((tm + N_META, pw), F32), pltpu.VMEM((tm, pw), BF16)],
        compiler_params=pltpu.CompilerParams(
            dimension_semantics=("arbitrary", "arbitrary"), vmem_limit_bytes=VMEM_LIMIT),
        name="mix_ln",
    )(a, u, u, um, h, wp, ps, wo, g, b)


def kernel(x, meta_tokens, ln1_g, ln1_b, ffn1_w_gate, ffn1_w_up, ffn1_w_down, w_in, lambda_q1, lambda_k1,
           lambda_q2, lambda_k2, subln_g, w_pool, pool_scale, w_out, ln2_g, ln2_b, ffn2_w_gate, ffn2_w_up,
           ffn2_w_down, ln3_g, ln3_b):
    bsz, s, d = x.shape
    l = 0
    tm, tf, tq = 512, 512, 256

    wg1, wu1, wd1 = (w[l].astype(BF16) for w in (ffn1_w_gate, ffn1_w_up, ffn1_w_down))
    wg2, wu2, wd2 = (w[l].astype(BF16) for w in (ffn2_w_gate, ffn2_w_up, ffn2_w_down))
    w_in_b = w_in[l].astype(BF16)
    w_pool_b = w_pool[l].astype(BF16)
    w_out_b = w_out[l].astype(BF16)
    row = lambda p: p[l].reshape(1, -1).astype(F32)

    h1, h1b = _ffn_ln(x.reshape(bsz * s, d), wg1, wu1, wd1, row(ln1_g), row(ln1_b), tm=tm, tf=tf)
    _, m1b = _ffn_ln(meta_tokens.astype(F32), wg1, wu1, wd1, row(ln1_g), row(ln1_b), tm=N_META, tf=tf)
    q, k, v, u = _in_proj(h1b, w_in_b, tm=tm)
    _, km, vm, um = _in_proj(m1b, w_in_b, tm=N_META)

    aw = N_HEADS * V_DIM
    slopes = 2.0 ** (-8.0 * jnp.arange(1, N_HEADS + 1, dtype=F32) / N_HEADS)
    slopes = jnp.broadcast_to(slopes[:, None, None], (N_HEADS, 1, V_DIM))
    a_out = _diff_attn(q.reshape(bsz, s, aw), k.reshape(bsz, s, aw), v.reshape(bsz, s, aw), km, vm, slopes,
                       row(lambda_q1), row(lambda_k1), row(lambda_q2), row(lambda_k2), row(subln_g), tq=tq)

    h2 = _mix_ln(a_out, u.reshape(bsz, s, -1), um, h1.reshape(bsz, s, d), w_pool_b, row(pool_scale), w_out_b,
                 row(ln2_g), row(ln2_b), tm=tm)

    out, _ = _ffn_ln(h2.reshape(bsz * s, d), wg2, wu2, wd2, row(ln3_g), row(ln3_b), tm=tm, tf=tf)
    return out.reshape(bsz, s, d)
```

```python
import functools
import math

import jax
import jax.numpy as jnp
from jax import lax
from jax.experimental import pallas as pl
from jax.experimental.pallas import tpu as pltpu

F32 = jnp.float32
BF16 = jnp.bfloat16

CHUNK = 64
N_META = 16
N_HEADS = 8
QK_DIM = 64
V_DIM = 128
POOL_WINDOWS = (2, 4, 8, 16)
POOL_GROUP = 256
LN_EPS = 1e-5
DEPTH = 1
ALPHA = (2.0 * DEPTH) ** 0.25
LAM_INIT = 0.8 - 0.6 * math.exp(-0.3 * 0)
NEG_INF = -1e30

VMEM_LIMIT = 56 * 1024 * 1024


def _layer_norm(y, g, b):
    mu = jnp.mean(y, axis=-1, keepdims=True)
    yc = y - mu
    var = jnp.mean(yc * yc, axis=-1, keepdims=True)
    return yc * lax.rsqrt(var + LN_EPS) * g + b


def _ffn_ln_kernel(h_ref, wg_ref, wu_ref, wd_ref, g_ref, b_ref, o_ref, ob_ref, hb_ref, acc_ref):
    f = pl.program_id(1)
    nf = pl.num_programs(1)

    @pl.when(f == 0)
    def _():
        hb_ref[...] = h_ref[...].astype(BF16)

    hb = hb_ref[...]
    gate = jnp.dot(hb, wg_ref[...], preferred_element_type=F32)
    up = jnp.dot(hb, wu_ref[...], preferred_element_type=F32)
    act = (gate / (1.0 + jnp.exp(-gate)) * up).astype(BF16)
    part = jnp.dot(act, wd_ref[...], preferred_element_type=F32)

    @pl.when(f == 0)
    def _():
        acc_ref[...] = part

    @pl.when(f > 0)
    def _():
        acc_ref[...] += part

    @pl.when(f == nf - 1)
    def _():
        y = ALPHA * h_ref[...] + 0.5 * acc_ref[...]
        out = _layer_norm(y, g_ref[...], b_ref[...])
        o_ref[...] = out
        ob_ref[...] = out.astype(BF16)


def _ffn_ln(h, wg, wu, wd, g, b, *, tm, tf):
    n, d = h.shape
    ff = wg.shape[1]
    grid = (n // tm, ff // tf)
    return pl.pallas_call(
        _ffn_ln_kernel,
        out_shape=(jax.ShapeDtypeStruct((n, d), F32), jax.ShapeDtypeStruct((n, d), BF16)),
        grid=grid,
        in_specs=[
            pl.BlockSpec((tm, d), lambda i, f: (i, 0)),
            pl.BlockSpec((d, tf), lambda i, f: (0, f)),
            pl.BlockSpec((d, tf), lambda i, f: (0, f)),
            pl.BlockSpec((tf, d), lambda i, f: (f, 0)),
            pl.BlockSpec((1, d), lambda i, f: (0, 0)),
            pl.BlockSpec((1, d), lambda i, f: (0, 0)),
        ],
        out_specs=(
            pl.BlockSpec((tm, d), lambda i, f: (i, 0)),
            pl.BlockSpec((tm, d), lambda i, f: (i, 0)),
        ),
        scratch_shapes=[pltpu.VMEM((tm, d), BF16), pltpu.VMEM((tm, d), F32)],
        compiler_params=pltpu.CompilerParams(
            dimension_semantics=("arbitrary", "arbitrary"), vmem_limit_bytes=VMEM_LIMIT),
        name="ffn_ln",
    )(h, wg, wu, wd, g, b)


def _in_proj_kernel(x_ref, w_ref, q_ref, k_ref, v_ref, u_ref):
    x = x_ref[...]
    aw = q_ref.shape[1]
    scale = QK_DIM ** -0.5
    q_ref[...] = (jnp.dot(x, w_ref[:, 0:aw], preferred_element_type=F32) * scale).astype(BF16)
    k_ref[...] = jnp.dot(x, w_ref[:, aw:2 * aw], preferred_element_type=F32).astype(BF16)
    v_ref[...] = jnp.dot(x, w_ref[:, 2 * aw:3 * aw], preferred_element_type=F32).astype(BF16)
    u_ref[...] = jnp.dot(x, w_ref[:, 3 * aw:], preferred_element_type=F32)


def _in_proj(x, w, *, tm):
    n, d = x.shape
    aw = N_HEADS * V_DIM
    pw = w.shape[1] - 3 * aw
    row = lambda i: (i, 0)
    return pl.pallas_call(
        _in_proj_kernel,
        out_shape=(
            jax.ShapeDtypeStruct((n, aw), BF16),
            jax.ShapeDtypeStruct((n, aw), BF16),
            jax.ShapeDtypeStruct((n, aw), BF16),
            jax.ShapeDtypeStruct((n, pw), F32),
        ),
        grid=(n // tm,),
        in_specs=[
            pl.BlockSpec((tm, d), row),
            pl.BlockSpec(w.shape, lambda i: (0, 0), pipeline_mode=pl.Buffered(1)),
        ],
        out_specs=(
            pl.BlockSpec((tm, aw), row),
            pl.BlockSpec((tm, aw), row),
            pl.BlockSpec((tm, aw), row),
            pl.BlockSpec((tm, pw), row),
        ),
        compiler_params=pltpu.CompilerParams(
            dimension_semantics=("arbitrary",), vmem_limit_bytes=VMEM_LIMIT),
        name="in_proj",
    )(x, w)


def _attn_kernel(q_ref, k_ref, v_ref, km_ref, vm_ref, slope_ref, lq1_ref, lk1_ref, lq2_ref, lk2_ref,
                 g_ref, o_ref, qs_ref, m_ref, l_ref, acc_ref, boff_ref, bdiag_ref, *, tq):
    qi = pl.program_id(2)
    cols = 2 * tq
    slope1 = slope_ref[0][:, 0:1]
    nt = (((1,), (1,)), ((), ()))
    tn = (((0,), (0,)), ((), ()))

    def query_index(shape):
        c = lax.broadcasted_iota(jnp.int32, shape, 1)
        return jnp.where(c >= tq, c - tq, c)

    @pl.when(qi == 0)
    def _():
        i = query_index((tq, cols))
        j = lax.broadcasted_iota(jnp.int32, (tq, cols), 0)
        rel = (i - j).astype(F32)
        boff_ref[...] = -slope1 * rel
        visible = (i // CHUNK) >= (j // CHUNK)
        bdiag_ref[...] = jnp.where(visible, -slope1 * jnp.abs(rel), NEG_INF)

    q = q_ref[0]
    lane = lax.broadcasted_iota(jnp.int32, q.shape, 1)
    zero = jnp.zeros_like(q)
    qs_ref[0:tq, :] = jnp.where(lane < QK_DIM, q, zero)
    qs_ref[tq:cols, :] = jnp.where(lane >= QK_DIM, q, zero)
    qs = qs_ref[...]

    s = lax.dot_general(km_ref[...], qs, nt, preferred_element_type=F32)
    i = query_index((N_META, cols))
    j = lax.broadcasted_iota(jnp.int32, (N_META, cols), 0)
    s = s - slope1 * (i - j + (N_META + qi * tq)).astype(F32)
    m0 = jnp.max(s, axis=0, keepdims=True)
    p = jnp.exp(s - m0)
    m_ref[...] = m0
    l_ref[...] = jnp.sum(p, axis=0, keepdims=True)
    acc_ref[...] = lax.dot_general(vm_ref[...], p.astype(BF16), tn, preferred_element_type=F32)

    def update(k_blk, v_blk, bias, shift):
        s = lax.dot_general(k_blk, qs, nt, preferred_element_type=F32) + bias
        m_prev = m_ref[...]
        m_cur = jnp.max(s, axis=0, keepdims=True) + shift
        m_new = jnp.maximum(m_prev, m_cur)
        p = jnp.exp(s - (m_new - shift))
        alpha = jnp.exp(m_prev - m_new)
        l_ref[...] = alpha * l_ref[...] + jnp.sum(p, axis=0, keepdims=True)
        acc_ref[...] = alpha * acc_ref[...] + lax.dot_general(
            v_blk, p.astype(BF16), tn, preferred_element_type=F32)
        m_ref[...] = m_new

    def body(ki, carry):
        start = pl.multiple_of(ki * tq, tq)
        k_blk = k_ref[0, pl.ds(start, tq), :]
        v_blk = v_ref[0, pl.ds(start, tq), :]
        gap = jnp.full((1, 1), (qi - ki) * tq, jnp.int32).astype(F32)
        update(k_blk, v_blk, boff_ref[...], -slope1 * gap)
        return carry

    lax.fori_loop(0, qi, body, 0)

    start = pl.multiple_of(qi * tq, tq)
    update(k_ref[0, pl.ds(start, tq), :], v_ref[0, pl.ds(start, tq), :], bdiag_ref[...],
           jnp.zeros((1, 1), F32))

    lam = (jnp.exp(jnp.sum(lq1_ref[...] * lk1_ref[...], axis=1, keepdims=True))
           - jnp.exp(jnp.sum(lq2_ref[...] * lk2_ref[...], axis=1, keepdims=True)) + LAM_INIT)
    o = acc_ref[...] / l_ref[...]
    d = o[:, 0:tq] - lam * o[:, tq:cols]
    d = d * lax.rsqrt(jnp.mean(d * d, axis=0, keepdims=True) + LN_EPS)
    o_ref[0] = (d.T * g_ref[...] * (1.0 - LAM_INIT)).astype(o_ref.dtype)


def _diff_attn(q, k, v, km, vm, slopes, lq1, lk1, lq2, lk2, g, *, tq):
    b, s, _ = q.shape
    cols = 2 * tq
    small = lambda shape: pl.BlockSpec(shape, lambda bi, h, i: (0,) * len(shape))
    return pl.pallas_call(
        functools.partial(_attn_kernel, tq=tq),
        out_shape=jax.ShapeDtypeStruct((b, s, N_HEADS * V_DIM), BF16),
        grid=(b, N_HEADS, s // tq),
        in_specs=[
            pl.BlockSpec((1, tq, V_DIM), lambda bi, h, i: (bi, i, h)),
            pl.BlockSpec((1, s, V_DIM), lambda bi, h, i: (bi, 0, h)),
            pl.BlockSpec((1, s, V_DIM), lambda bi, h, i: (bi, 0, h)),
            pl.BlockSpec((N_META, V_DIM), lambda bi, h, i: (0, h)),
            pl.BlockSpec((N_META, V_DIM), lambda bi, h, i: (0, h)),
            pl.BlockSpec((1, 1, V_DIM), lambda bi, h, i: (h, 0, 0)),
            small((1, QK_DIM)), small((1, QK_DIM)), small((1, QK_DIM)), small((1, QK_DIM)),
            small((1, V_DIM)),
        ],
        out_specs=pl.BlockSpec((1, tq, V_DIM), lambda bi, h, i: (bi, i, h)),
        scratch_shapes=[
            pltpu.VMEM((cols, V_DIM), BF16),
            pltpu.VMEM((1, cols), F32),
            pltpu.VMEM((1, cols), F32),
            pltpu.VMEM((V_DIM, cols), F32),
            pltpu.VMEM((tq, cols), F32),
            pltpu.VMEM((tq, cols), F32),
        ],
        compiler_params=pltpu.CompilerParams(
            dimension_semantics=("arbitrary", "arbitrary", "arbitrary"), vmem_limit_bytes=VMEM_LIMIT),
        name="diff_attn",
    )(q, k, v, km, vm, slopes, lq1, lk1, lq2, lk2, g)


def _mix_ln_kernel(a_ref, u_ref, halo_ref, um_ref, h_ref, wp_ref, ps_ref, wo_ref, g_ref, b_ref,
                   o_ref, ext_ref, p_ref, *, tm):
    i = pl.program_id(1)
    hw = N_META

    @pl.when(i == 0)
    def _():
        ext_ref[0:hw, :] = um_ref[...]

    @pl.when(i > 0)
    def _():
        ext_ref[0:hw, :] = halo_ref[0]

    ext_ref[hw:hw + tm, :] = u_ref[0]

    for gi, w in enumerate(POOL_WINDOWS):
        cols = slice(gi * POOL_GROUP, (gi + 1) * POOL_GROUP)
        cur = ext_ref[hw:hw + tm, cols]
        win = cur
        for back in range(1, w):
            win = win + ext_ref[hw - back:hw - back + tm, cols]
        pooled = win * (1.0 / w) - cur
        y = jnp.dot(pooled.astype(BF16), wp_ref[gi], preferred_element_type=F32)
        p_ref[:, cols] = (y * ps_ref[:, cols]).astype(BF16)

    aw = a_ref.shape[2]
    mix = jnp.dot(a_ref[0], wo_ref[0:aw, :], preferred_element_type=F32)
    mix = mix + jnp.dot(p_ref[...], wo_ref[aw:, :], preferred_element_type=F32)
    o_ref[0] = _layer_norm(ALPHA * h_ref[0] + mix, g_ref[...], b_ref[...])


def _mix_ln(a, u, um, h, wp, ps, wo, g, b, *, tm):
    bsz, s, d = h.shape
    aw = a.shape[2]
    pw = u.shape[2]
    hb = tm // N_META
    const2 = lambda shape: pl.BlockSpec(shape, lambda bi, i: (0,) * len(shape), pipeline_mode=pl.Buffered(1))
    return pl.pallas_call(
        functools.partial(_mix_ln_kernel, tm=tm),
        out_shape=jax.ShapeDtypeStruct((bsz, s, d), F32),
        grid=(bsz, s // tm),
        in_specs=[
            pl.BlockSpec((1, tm, aw), lambda bi, i: (bi, i, 0)),
            pl.BlockSpec((1, tm, pw), lambda bi, i: (bi, i, 0)),
            pl.BlockSpec((1, N_META, pw), lambda bi, i: (bi, jnp.maximum(i * hb - 1, 0), 0)),
            const2((N_META, pw)),
            pl.BlockSpec((1, tm, d), lambda bi, i: (bi, i, 0)),
            const2(wp.shape),
            const2((1, pw)),
            const2(wo.shape),
            const2((1, d)),
            const2((1, d)),
        ],
        out_specs=pl.BlockSpec((1, tm, d), lambda bi, i: (bi, i, 0)),
        scratch_shapes=[pltpu.VMEM((tm + N_META, pw), F32), pltpu.VMEM((tm, pw), BF16)],
        compiler_params=pltpu.CompilerParams(
            dimension_semantics=("arbitrary", "arbitrary"), vmem_limit_bytes=VMEM_LIMIT),
        name="mix_ln",
    )(a, u, u, um, h, wp, ps, wo, g, b)


def kernel(x, meta_tokens, ln1_g, ln1_b, ffn1_w_gate, ffn1_w_up, ffn1_w_down, w_in, lambda_q1, lambda_k1,
           lambda_q2, lambda_k2, subln_g, w_pool, pool_scale, w_out, ln2_g, ln2_b, ffn2_w_gate, ffn2_w_up,
           ffn2_w_down, ln3_g, ln3_b):
    bsz, s, d = x.shape
    l = 0
    tm, tf, tq = 512, 512, 256

    wg1, wu1, wd1 = (w[l].astype(BF16) for w in (ffn1_w_gate, ffn1_w_up, ffn1_w_down))
    wg2, wu2, wd2 = (w[l].astype(BF16) for w in (ffn2_w_gate, ffn2_w_up, ffn2_w_down))
    w_in_b = w_in[l].astype(BF16)
    w_pool_b = w_pool[l].astype(BF16)
    w_out_b = w_out[l].astype(BF16)
    row = lambda p: p[l].reshape(1, -1).astype(F32)

    h1, h1b = _ffn_ln(x.reshape(bsz * s, d), wg1, wu1, wd1, row(ln1_g), row(ln1_b), tm=tm, tf=tf)
    _, m1b = _ffn_ln(meta_tokens.astype(F32), wg1, wu1, wd1, row(ln1_g), row(ln1_b), tm=N_META, tf=tf)
    q, k, v, u = _in_proj(h1b, w_in_b, tm=tm)
    _, km, vm, um = _in_proj(m1b, w_in_b, tm=N_META)

    aw = N_HEADS * V_DIM
    slopes = 2.0 ** (-8.0 * jnp.arange(1, N_HEADS + 1, dtype=F32) / N_HEADS)
    slopes = jnp.broadcast_to(slopes[:, None, None], (N_HEADS, 1, V_DIM))
    a_out = _diff_attn(q.reshape(bsz, s, aw), k.reshape(bsz, s, aw), v.reshape(bsz, s, aw), km, vm, slopes,
                       row(lambda_q1), row(lambda_k1), row(lambda_q2), row(lambda_k2), row(subln_g), tq=tq)

    h2 = _mix_ln(a_out, u.reshape(bsz, s, -1), um, h1.reshape(bsz, s, d), w_pool_b, row(pool_scale), w_out_b,
                 row(ln2_g), row(ln2_b), tm=tm)

    out, _ = _ffn_ln(h2.reshape(bsz * s, d), wg2, wu2, wd2, row(ln3_g), row(ln3_b), tm=tm, tf=tf)
    return out.reshape(bsz, s, d)
```

```python
import functools
import math

import jax
import jax.numpy as jnp
from jax import lax
from jax.experimental import pallas as pl
from jax.experimental.pallas import tpu as pltpu

F32 = jnp.float32
BF16 = jnp.bfloat16

CHUNK = 64
N_META = 16
N_HEADS = 8
QK_DIM = 64
V_DIM = 128
POOL_WINDOWS = (2, 4, 8, 16)
POOL_GROUP = 256
LN_EPS = 1e-5
DEPTH = 1
ALPHA = (2.0 * DEPTH) ** 0.25
LAM_INIT = 0.8 - 0.6 * math.exp(-0.3 * 0)
NEG_INF = -1e30
STRIP = 32
ONES_ROWS = 16

VMEM_LIMIT = 56 * 1024 * 1024


def _layer_norm(y, g, b):
    mu = jnp.mean(y, axis=-1, keepdims=True)
    yc = y - mu
    var = jnp.mean(yc * yc, axis=-1, keepdims=True)
    return yc * lax.rsqrt(var + LN_EPS) * g + b


def _ffn_ln_kernel(h_ref, wg_ref, wu_ref, wd_ref, g_ref, b_ref, o_ref, ob_ref, hb_ref, acc_ref):
    f = pl.program_id(1)
    nf = pl.num_programs(1)

    @pl.when(f == 0)
    def _():
        hb_ref[...] = h_ref[...].astype(BF16)

    hb = hb_ref[...]
    gate = jnp.dot(hb, wg_ref[...], preferred_element_type=F32)
    up = jnp.dot(hb, wu_ref[...], preferred_element_type=F32)
    act = (gate / (1.0 + jnp.exp(-gate)) * up).astype(BF16)
    part = jnp.dot(act, wd_ref[...], preferred_element_type=F32)

    @pl.when(f == 0)
    def _():
        acc_ref[...] = part

    @pl.when(f > 0)
    def _():
        acc_ref[...] += part

    @pl.when(f == nf - 1)
    def _():
        y = ALPHA * h_ref[...] + 0.5 * acc_ref[...]
        out = _layer_norm(y, g_ref[...], b_ref[...])
        o_ref[...] = out
        ob_ref[...] = out.astype(BF16)


def _ffn_ln(h, wg, wu, wd, g, b, *, tm, tf):
    n, d = h.shape
    ff = wg.shape[1]
    grid = (n // tm, ff // tf)
    return pl.pallas_call(
        _ffn_ln_kernel,
        out_shape=(jax.ShapeDtypeStruct((n, d), F32), jax.ShapeDtypeStruct((n, d), BF16)),
        grid=grid,
        in_specs=[
            pl.BlockSpec((tm, d), lambda i, f: (i, 0)),
            pl.BlockSpec((d, tf), lambda i, f: (0, f)),
            pl.BlockSpec((d, tf), lambda i, f: (0, f)),
            pl.BlockSpec((tf, d), lambda i, f: (f, 0)),
            pl.BlockSpec((1, d), lambda i, f: (0, 0)),
            pl.BlockSpec((1, d), lambda i, f: (0, 0)),
        ],
        out_specs=(
            pl.BlockSpec((tm, d), lambda i, f: (i, 0)),
            pl.BlockSpec((tm, d), lambda i, f: (i, 0)),
        ),
        scratch_shapes=[pltpu.VMEM((tm, d), BF16), pltpu.VMEM((tm, d), F32)],
        compiler_params=pltpu.CompilerParams(
            dimension_semantics=("arbitrary", "arbitrary"), vmem_limit_bytes=VMEM_LIMIT),
        name="ffn_ln",
    )(h, wg, wu, wd, g, b)


def _in_proj_kernel(x_ref, w_ref, q_ref, k_ref, v_ref, u_ref):
    x = x_ref[...]
    aw = q_ref.shape[1]
    scale = QK_DIM ** -0.5
    q_ref[...] = (jnp.dot(x, w_ref[:, 0:aw], preferred_element_type=F32) * scale).astype(BF16)
    k_ref[...] = jnp.dot(x, w_ref[:, aw:2 * aw], preferred_element_type=F32).astype(BF16)
    v_ref[...] = jnp.dot(x, w_ref[:, 2 * aw:3 * aw], preferred_element_type=F32).astype(BF16)
    u_ref[...] = jnp.dot(x, w_ref[:, 3 * aw:], preferred_element_type=F32)


def _in_proj(x, w, *, tm):
    n, d = x.shape
    aw = N_HEADS * V_DIM
    pw = w.shape[1] - 3 * aw
    row = lambda i: (i, 0)
    return pl.pallas_call(
        _in_proj_kernel,
        out_shape=(
            jax.ShapeDtypeStruct((n, aw), BF16),
            jax.ShapeDtypeStruct((n, aw), BF16),
            jax.ShapeDtypeStruct((n, aw), BF16),
            jax.ShapeDtypeStruct((n, pw), F32),
        ),
        grid=(n // tm,),
        in_specs=[
            pl.BlockSpec((tm, d), row),
            pl.BlockSpec(w.shape, lambda i: (0, 0), pipeline_mode=pl.Buffered(1)),
        ],
        out_specs=(
            pl.BlockSpec((tm, aw), row),
            pl.BlockSpec((tm, aw), row),
            pl.BlockSpec((tm, aw), row),
            pl.BlockSpec((tm, pw), row),
        ),
        compiler_params=pltpu.CompilerParams(
            dimension_semantics=("arbitrary",), vmem_limit_bytes=VMEM_LIMIT),
        name="in_proj",
    )(x, w)


def _attn_kernel(q_ref, k_ref, v_ref, km_ref, vm_ref, slope_ref, lq1_ref, lk1_ref, lq2_ref, lk2_ref,
                 g_ref, o_ref, qs_ref, vt_ref, m_ref, acc_ref, bias_ref,
                 s0_ref, s1_ref, c0_ref, c1_ref, p0_ref, p1_ref, a0_ref, a1_ref, *, tq, tk):
    qi = pl.program_id(2)
    cols = 2 * tq
    nd = tq // tk
    n_off = nd * qi
    slope1 = slope_ref[0][:, 0:1]
    tn = (((0,), (0,)), ((), ()))
    s_refs, c_refs, p_refs, a_refs = (s0_ref, s1_ref), (c0_ref, c1_ref), (p0_ref, p1_ref), (a0_ref, a1_ref)

    def query_index(shape):
        c = lax.broadcasted_iota(jnp.int32, shape, 1)
        return jnp.where(c >= tq, c - tq, c)

    @pl.when(qi == 0)
    def _():
        i = query_index((tk, cols))
        j = lax.broadcasted_iota(jnp.int32, (tk, cols), 0)
        bias_ref[0] = -slope1 * (i - j).astype(F32)
        for t in range(nd):
            jt = j + t * tk
            visible = (i // CHUNK) >= (jt // CHUNK)
            bias_ref[1 + t] = jnp.where(visible, -slope1 * jnp.abs(i - jt).astype(F32), NEG_INF)

        def transpose_values(c, carry):
            start = pl.multiple_of(c * tk, tk)
            vt_ref[c, 0:V_DIM, :] = v_ref[0, pl.ds(start, tk), :].astype(F32).T.astype(BF16)
            vt_ref[c, V_DIM:V_DIM + ONES_ROWS, :] = jnp.ones((ONES_ROWS, tk), BF16)
            return carry

        lax.fori_loop(0, v_ref.shape[1] // tk, transpose_values, 0)

    qt = q_ref[0].astype(F32).T
    row = lax.broadcasted_iota(jnp.int32, qt.shape, 0)
    qs_ref[:, 0:tq] = jnp.where(row < QK_DIM, qt, 0.0).astype(BF16)
    qs_ref[:, tq:cols] = jnp.where(row >= QK_DIM, qt, 0.0).astype(BF16)

    s = jnp.dot(km_ref[...], qs_ref[...], preferred_element_type=F32)
    i = query_index((N_META, cols))
    j = lax.broadcasted_iota(jnp.int32, (N_META, cols), 0)
    s = s - slope1 * (i - j + (N_META + qi * tq)).astype(F32)
    m0 = jnp.max(s, axis=0, keepdims=True)
    p = jnp.exp(s - m0)
    pb = p.astype(BF16)
    m_ref[...] = m0
    acc_ref[0:V_DIM, :] = lax.dot_general(vm_ref[...], pb, tn, preferred_element_type=F32)
    acc_ref[V_DIM:V_DIM + ONES_ROWS, :] = jnp.broadcast_to(
        jnp.sum(pb.astype(F32), axis=0, keepdims=True), (ONES_ROWS, cols))

    def scores(b, slot):
        start = pl.multiple_of(b * tk, tk)
        table = jnp.where(b < n_off, 0, b - n_off + 1)
        s = jnp.dot(k_ref[0, pl.ds(start, tk), :], qs_ref[...], preferred_element_type=F32)
        s = s + bias_ref[table]
        s_refs[slot][...] = s
        c_refs[slot][...] = jnp.max(s, axis=0, keepdims=True)

    def softmax(b, slot):
        gap = jnp.where(b < n_off, qi * tq - b * tk, 0)
        shift = -slope1 * jnp.full((1, 1), gap, jnp.int32).astype(F32)
        m_prev = m_ref[...]
        m_new = jnp.maximum(m_prev, c_refs[slot][...] + shift)
        offset = m_new - shift
        for r in range(0, tk, STRIP):
            p = jnp.exp(s_refs[slot][r:r + STRIP, :] - offset)
            p_refs[slot][r:r + STRIP, :] = p.astype(BF16)
        a_refs[slot][...] = jnp.exp(m_prev - m_new)
        m_ref[...] = m_new

    def values(b, slot):
        pv = jnp.dot(vt_ref[b], p_refs[slot][...], preferred_element_type=F32)
        acc_ref[...] = a_refs[slot][...] * acc_ref[...] + pv

    scores(0, 0)
    scores(1, 1)
    softmax(0, 0)

    def pair(t, carry):
        b = 2 * t
        scores(b, 0)
        softmax(b - 1, 1)
        values(b - 2, 0)
        scores(b + 1, 1)
        softmax(b, 0)
        values(b - 1, 1)
        return carry

    n_blocks = n_off + nd
    lax.fori_loop(1, n_blocks // 2, pair, 0)
    softmax(n_blocks - 1, 1)
    values(n_blocks - 2, 0)
    values(n_blocks - 1, 1)

    lam = (jnp.exp(jnp.sum(lq1_ref[...] * lk1_ref[...], axis=1, keepdims=True))
           - jnp.exp(jnp.sum(lq2_ref[...] * lk2_ref[...], axis=1, keepdims=True)) + LAM_INIT)
    o = acc_ref[0:V_DIM, :] / acc_ref[V_DIM:V_DIM + 1, :]
    d = o[:, 0:tq] - lam * o[:, tq:cols]
    d = d * lax.rsqrt(jnp.mean(d * d, axis=0, keepdims=True) + LN_EPS)
    o_ref[0] = (d.T * g_ref[...] * (1.0 - LAM_INIT)).astype(o_ref.dtype)


def _diff_attn(q, k, v, km, vm, slopes, lq1, lk1, lq2, lk2, g, *, tq, tk):
    b, s, _ = q.shape
    cols = 2 * tq
    assert tq == 2 * tk, "the pipeline below walks key blocks in pairs"
    small = lambda shape: pl.BlockSpec(shape, lambda bi, h, i: (0,) * len(shape))
    slot = lambda shape, dtype: [pltpu.VMEM(shape, dtype), pltpu.VMEM(shape, dtype)]
    return pl.pallas_call(
        functools.partial(_attn_kernel, tq=tq, tk=tk),
        out_shape=jax.ShapeDtypeStruct((b, s, N_HEADS * V_DIM), BF16),
        grid=(b, N_HEADS, s // tq),
        in_specs=[
            pl.BlockSpec((1, tq, V_DIM), lambda bi, h, i: (bi, i, h)),
            pl.BlockSpec((1, s, V_DIM), lambda bi, h, i: (bi, 0, h)),
            pl.BlockSpec((1, s, V_DIM), lambda bi, h, i: (bi, 0, h)),
            pl.BlockSpec((N_META, V_DIM), lambda bi, h, i: (0, h)),
            pl.BlockSpec((N_META, V_DIM), lambda bi, h, i: (0, h)),
            pl.BlockSpec((1, 1, V_DIM), lambda bi, h, i: (h, 0, 0)),
            small((1, QK_DIM)), small((1, QK_DIM)), small((1, QK_DIM)), small((1, QK_DIM)),
            small((1, V_DIM)),
        ],
        out_specs=pl.BlockSpec((1, tq, V_DIM), lambda bi, h, i: (bi, i, h)),
        scratch_shapes=[
            pltpu.VMEM((V_DIM, cols), BF16),
            pltpu.VMEM((s // tk, V_DIM + ONES_ROWS, tk), BF16),
            pltpu.VMEM((1, cols), F32),
            pltpu.VMEM((V_DIM + ONES_ROWS, cols), F32),
            pltpu.VMEM((1 + tq // tk, tk, cols), F32),
            *slot((tk, cols), F32),
            *slot((1, cols), F32),
            *slot((tk, cols), BF16),
            *slot((1, cols), F32),
        ],
        compiler_params=pltpu.CompilerParams(
            dimension_semantics=("arbitrary", "arbitrary", "arbitrary"), vmem_limit_bytes=VMEM_LIMIT),
        name="diff_attn",
    )(q, k, v, km, vm, slopes, lq1, lk1, lq2, lk2, g)


def _mix_ln_kernel(a_ref, u_ref, halo_ref, um_ref, h_ref, wp_ref, ps_ref, wo_ref, g_ref, b_ref,
                   o_ref, ext_ref, p_ref, *, tm):
    i = pl.program_id(1)
    hw = N_META

    @pl.when(i == 0)
    def _():
        ext_ref[0:hw, :] = um_ref[...]

    @pl.when(i > 0)
    def _():
        ext_ref[0:hw, :] = halo_ref[0]

    ext_ref[hw:hw + tm, :] = u_ref[0]

    for gi, w in enumerate(POOL_WINDOWS):
        cols = slice(gi * POOL_GROUP, (gi + 1) * POOL_GROUP)
        cur = ext_ref[hw:hw + tm, cols]
        win = cur
        for back in range(1, w):
            win = win + ext_ref[hw - back:hw - back + tm, cols]
        pooled = win * (1.0 / w) - cur
        y = jnp.dot(pooled.astype(BF16), wp_ref[gi], preferred_element_type=F32)
        p_ref[:, cols] = (y * ps_ref[:, cols]).astype(BF16)

    aw = a_ref.shape[2]
    mix = jnp.dot(a_ref[0], wo_ref[0:aw, :], preferred_element_type=F32)
    mix = mix + jnp.dot(p_ref[...], wo_ref[aw:, :], preferred_element_type=F32)
    o_ref[0] = _layer_norm(ALPHA * h_ref[0] + mix, g_ref[...], b_ref[...])


def _mix_ln(a, u, um, h, wp, ps, wo, g, b, *, tm):
    bsz, s, d = h.shape
    aw = a.shape[2]
    pw = u.shape[2]
    hb = tm // N_META
    const2 = lambda shape: pl.BlockSpec(shape, lambda bi, i: (0,) * len(shape), pipeline_mode=pl.Buffered(1))
    return pl.pallas_call(
        functools.partial(_mix_ln_kernel, tm=tm),
        out_shape=jax.ShapeDtypeStruct((bsz, s, d), F32),
        grid=(bsz, s // tm),
        in_specs=[
            pl.BlockSpec((1, tm, aw), lambda bi, i: (bi, i, 0)),
            pl.BlockSpec((1, tm, pw), lambda bi, i: (bi, i, 0)),
            pl.BlockSpec((1, N_META, pw), lambda bi, i: (bi, jnp.maximum(i * hb - 1, 0), 0)),
            const2((N_META, pw)),
            pl.BlockSpec((1, tm, d), lambda bi, i: (bi, i, 0)),
            const2(wp.shape),
            const2((1, pw)),
            const2(wo.shape),
            const2((1, d)),
            const2((1, d)),
        ],
        out_specs=pl.BlockSpec((1, tm, d), lambda bi, i: (bi, i, 0)),
        scratch_shapes=[pltpu.VMEM((tm + N_META, pw), F32), pltpu.VMEM((tm, pw), BF16)],
        compiler_params=pltpu.CompilerParams(
            dimension_semantics=("arbitrary", "arbitrary"), vmem_limit_bytes=VMEM_LIMIT),
        name="mix_ln",
    )(a, u, u, um, h, wp, ps, wo, g, b)


def kernel(x, meta_tokens, ln1_g, ln1_b, ffn1_w_gate, ffn1_w_up, ffn1_w_down, w_in, lambda_q1, lambda_k1,
           lambda_q2, lambda_k2, subln_g, w_pool, pool_scale, w_out, ln2_g, ln2_b, ffn2_w_gate, ffn2_w_up,
           ffn2_w_down, ln3_g, ln3_b):
    bsz, s, d = x.shape
    l = 0
    tm, tf, tq, tk = 512, 512, 512, 256

    wg1, wu1, wd1 = (w[l].astype(BF16) for w in (ffn1_w_gate, ffn1_w_up, ffn1_w_down))
    wg2, wu2, wd2 = (w[l].astype(BF16) for w in (ffn2_w_gate, ffn2_w_up, ffn2_w_down))
    w_in_b = w_in[l].astype(BF16)
    w_pool_b = w_pool[l].astype(BF16)
    w_out_b = w_out[l].astype(BF16)
    row = lambda p: p[l].reshape(1, -1).astype(F32)

    h1, h1b = _ffn_ln(x.reshape(bsz * s, d), wg1, wu1, wd1, row(ln1_g), row(ln1_b), tm=tm, tf=tf)
    _, m1b = _ffn_ln(meta_tokens.astype(F32), wg1, wu1, wd1, row(ln1_g), row(ln1_b), tm=N_META, tf=tf)
    q, k, v, u = _in_proj(h1b, w_in_b, tm=tm)
    _, km, vm, um = _in_proj(m1b, w_in_b, tm=N_META)

    aw = N_HEADS * V_DIM
    slopes = 2.0 ** (-8.0 * jnp.arange(1, N_HEADS + 1, dtype=F32) / N_HEADS)
    slopes = jnp.broadcast_to(slopes[:, None, None], (N_HEADS, 1, V_DIM))
    a_out = _diff_attn(q.reshape(bsz, s, aw), k.reshape(bsz, s, aw), v.reshape(bsz, s, aw), km, vm, slopes,
                       row(lambda_q1), row(lambda_k1), row(lambda_q2), row(lambda_k2), row(subln_g), tq=tq, tk=tk)

    h2 = _mix_ln(a_out, u.reshape(bsz, s, -1), um, h1.reshape(bsz, s, d), w_pool_b, row(pool_scale), w_out_b,
                 row(ln2_g), row(ln2_b), tm=tm)

    out, _ = _ffn_ln(h2.reshape(bsz * s, d), wg2, wu2, wd2, row(ln3_g), row(ln3_b), tm=tm, tf=tf)
    return out.reshape(bsz, s, d)
```

```python
import functools
import math

import jax
import jax.numpy as jnp
from jax import lax
from jax.experimental import pallas as pl
from jax.experimental.pallas import tpu as pltpu

F32 = jnp.float32
BF16 = jnp.bfloat16

CHUNK = 64
N_META = 16
N_HEADS = 8
QK_DIM = 64
V_DIM = 128
POOL_WINDOWS = (2, 4, 8, 16)
POOL_GROUP = 256
LN_EPS = 1e-5
DEPTH = 1
ALPHA = (2.0 * DEPTH) ** 0.25
LAM_INIT = 0.8 - 0.6 * math.exp(-0.3 * 0)
NEG_INF = -1e30
STRIP = 256
ONES_ROWS = 16

VMEM_LIMIT = 56 * 1024 * 1024


def _layer_norm(y, g, b):
    mu = jnp.mean(y, axis=-1, keepdims=True)
    yc = y - mu
    var = jnp.mean(yc * yc, axis=-1, keepdims=True)
    return yc * lax.rsqrt(var + LN_EPS) * g + b


def _ffn_ln_kernel(h_ref, wg_ref, wu_ref, wd_ref, g_ref, b_ref, o_ref, ob_ref, hb_ref, acc_ref):
    f = pl.program_id(1)
    nf = pl.num_programs(1)

    @pl.when(f == 0)
    def _():
        hb_ref[...] = h_ref[...].astype(BF16)
        acc_ref[...] = jnp.zeros_like(acc_ref)

    hb = hb_ref[...]
    gate = jnp.dot(hb, wg_ref[...], preferred_element_type=F32)
    up = jnp.dot(hb, wu_ref[...], preferred_element_type=F32)
    act = (gate / (1.0 + jnp.exp(-gate)) * up).astype(BF16)
    acc_ref[...] += jnp.dot(act, wd_ref[...], preferred_element_type=F32)

    @pl.when(f == nf - 1)
    def _():
        y = ALPHA * h_ref[...] + 0.5 * acc_ref[...]
        out = _layer_norm(y, g_ref[...], b_ref[...])
        o_ref[...] = out
        ob_ref[...] = out.astype(BF16)


def _ffn_ln(h, wg, wu, wd, g, b, *, tm, tf):
    n, d = h.shape
    ff = wg.shape[1]
    grid = (n // tm, ff // tf)
    return pl.pallas_call(
        _ffn_ln_kernel,
        out_shape=(jax.ShapeDtypeStruct((n, d), F32), jax.ShapeDtypeStruct((n, d), BF16)),
        grid=grid,
        in_specs=[
            pl.BlockSpec((tm, d), lambda i, f: (i, 0)),
            pl.BlockSpec((d, tf), lambda i, f: (0, f)),
            pl.BlockSpec((d, tf), lambda i, f: (0, f)),
            pl.BlockSpec((tf, d), lambda i, f: (f, 0)),
            pl.BlockSpec((1, d), lambda i, f: (0, 0)),
            pl.BlockSpec((1, d), lambda i, f: (0, 0)),
        ],
        out_specs=(
            pl.BlockSpec((tm, d), lambda i, f: (i, 0)),
            pl.BlockSpec((tm, d), lambda i, f: (i, 0)),
        ),
        scratch_shapes=[pltpu.VMEM((tm, d), BF16), pltpu.VMEM((tm, d), F32)],
        compiler_params=pltpu.CompilerParams(
            dimension_semantics=("arbitrary", "arbitrary"), vmem_limit_bytes=VMEM_LIMIT),
        name="ffn_ln",
    )(h, wg, wu, wd, g, b)


def _in_proj_kernel(x_ref, w_ref, q_ref, k_ref, v_ref, u_ref):
    x = x_ref[...]
    aw = q_ref.shape[1]
    scale = QK_DIM ** -0.5
    q_ref[...] = (jnp.dot(x, w_ref[:, 0:aw], preferred_element_type=F32) * scale).astype(BF16)
    k_ref[...] = jnp.dot(x, w_ref[:, aw:2 * aw], preferred_element_type=F32).astype(BF16)
    v_ref[...] = jnp.dot(x, w_ref[:, 2 * aw:3 * aw], preferred_element_type=F32).astype(BF16)
    u_ref[...] = jnp.dot(x, w_ref[:, 3 * aw:], preferred_element_type=F32)


def _in_proj(x, w, *, tm):
    n, d = x.shape
    aw = N_HEADS * V_DIM
    pw = w.shape[1] - 3 * aw
    row = lambda i: (i, 0)
    return pl.pallas_call(
        _in_proj_kernel,
        out_shape=(
            jax.ShapeDtypeStruct((n, aw), BF16),
            jax.ShapeDtypeStruct((n, aw), BF16),
            jax.ShapeDtypeStruct((n, aw), BF16),
            jax.ShapeDtypeStruct((n, pw), F32),
        ),
        grid=(n // tm,),
        in_specs=[
            pl.BlockSpec((tm, d), row),
            pl.BlockSpec(w.shape, lambda i: (0, 0), pipeline_mode=pl.Buffered(1)),
        ],
        out_specs=(
            pl.BlockSpec((tm, aw), row),
            pl.BlockSpec((tm, aw), row),
            pl.BlockSpec((tm, aw), row),
            pl.BlockSpec((tm, pw), row),
        ),
        compiler_params=pltpu.CompilerParams(
            dimension_semantics=("arbitrary",), vmem_limit_bytes=VMEM_LIMIT),
        name="in_proj",
    )(x, w)


def _attn_kernel(q_ref, k_ref, v_ref, km_ref, vm_ref, slope_ref, lq1_ref, lk1_ref, lq2_ref, lk2_ref,
                 g_ref, o_ref, qs_ref, vt_ref, m_ref, acc_ref, bias_ref,
                 s0_ref, s1_ref, c0_ref, c1_ref, p0_ref, p1_ref, a0_ref, a1_ref, *, tq, tk):
    qi = pl.program_id(2)
    cols = 2 * tq
    nd = tq // tk
    n_off = nd * qi
    slope = slope_ref[0]
    tn = (((0,), (0,)), ((), ()))
    s_refs, c_refs, p_refs, a_refs = (s0_ref, s1_ref), (c0_ref, c1_ref), (p0_ref, p1_ref), (a0_ref, a1_ref)

    def query_index(shape):
        c = lax.broadcasted_iota(jnp.int32, shape, 1)
        return jnp.where(c >= tq, c - tq, c)

    @pl.when(qi == 0)
    def _():
        i = query_index((tk, cols))
        j = lax.broadcasted_iota(jnp.int32, (tk, cols), 0)
        bias_ref[0] = -slope * (i - j).astype(F32)
        for t in range(nd):
            jt = j + t * tk
            visible = (i // CHUNK) >= (jt // CHUNK)
            bias_ref[1 + t] = jnp.where(visible, -slope * jnp.abs(i - jt).astype(F32), NEG_INF)

        def transpose_values(c, carry):
            start = pl.multiple_of(c * tk, tk)
            vt_ref[c, 0:V_DIM, :] = v_ref[0, pl.ds(start, tk), :].astype(F32).T.astype(BF16)
            vt_ref[c, V_DIM:V_DIM + ONES_ROWS, :] = jnp.ones((ONES_ROWS, tk), BF16)
            return carry

        lax.fori_loop(0, v_ref.shape[1] // tk, transpose_values, 0)

    qt = q_ref[0].astype(F32).T
    row = lax.broadcasted_iota(jnp.int32, qt.shape, 0)
    qs_ref[:, 0:tq] = jnp.where(row < QK_DIM, qt, 0.0).astype(BF16)
    qs_ref[:, tq:cols] = jnp.where(row >= QK_DIM, qt, 0.0).astype(BF16)

    s = jnp.dot(km_ref[...], qs_ref[...], preferred_element_type=F32)
    i = query_index((N_META, cols))
    j = lax.broadcasted_iota(jnp.int32, (N_META, cols), 0)
    s = s - slope * (i - j + (N_META + qi * tq)).astype(F32)
    m0 = jnp.max(s, axis=0, keepdims=True)
    p = jnp.exp(s - m0)
    pb = p.astype(BF16)
    m_ref[...] = m0
    acc_ref[0:V_DIM, :] = lax.dot_general(vm_ref[...], pb, tn, preferred_element_type=F32)
    acc_ref[V_DIM:V_DIM + ONES_ROWS, :] = jnp.broadcast_to(
        jnp.sum(pb.astype(F32), axis=0, keepdims=True), (ONES_ROWS, cols))

    def scores(b, slot):
        start = pl.multiple_of(b * tk, tk)
        table = jnp.where(b < n_off, 0, b - n_off + 1)
        s = jnp.dot(k_ref[0, pl.ds(start, tk), :], qs_ref[...], preferred_element_type=F32)
        s = s + bias_ref[table]
        s_refs[slot][...] = s
        c_refs[slot][...] = jnp.max(s, axis=0, keepdims=True)

    def softmax(b, slot):
        gap = jnp.where(b < n_off, qi * tq - b * tk, 0)
        shift = -slope * jnp.full((1, cols), gap, jnp.int32).astype(F32)
        m_prev = m_ref[...]
        m_new = jnp.maximum(m_prev, c_refs[slot][...] + shift)
        offset = m_new - shift
        for c in range(0, cols, STRIP):
            p = jnp.exp(s_refs[slot][:, c:c + STRIP] - offset[:, c:c + STRIP])
            p_refs[slot][:, c:c + STRIP] = p.astype(BF16)
        a_refs[slot][...] = jnp.exp(m_prev - m_new)
        m_ref[...] = m_new

    def values(b, slot):
        pv = jnp.dot(vt_ref[b], p_refs[slot][...], preferred_element_type=F32)
        acc_ref[...] = a_refs[slot][...] * acc_ref[...] + pv

    scores(0, 0)
    scores(1, 1)
    softmax(0, 0)

    def pair(t, carry):
        b = 2 * t
        softmax(b - 1, 1)
        values(b - 2, 0)
        scores(b, 0)
        softmax(b, 0)
        values(b - 1, 1)
        scores(b + 1, 1)
        return carry

    n_blocks = n_off + nd
    lax.fori_loop(1, n_blocks // 2, pair, 0)
    softmax(n_blocks - 1, 1)
    values(n_blocks - 2, 0)
    values(n_blocks - 1, 1)

    lam = (jnp.exp(jnp.sum(lq1_ref[...] * lk1_ref[...], axis=1, keepdims=True))
           - jnp.exp(jnp.sum(lq2_ref[...] * lk2_ref[...], axis=1, keepdims=True)) + LAM_INIT)
    o = acc_ref[0:V_DIM, :] / acc_ref[V_DIM:V_DIM + 1, :]
    d = o[:, 0:tq] - lam * o[:, tq:cols]
    d = d * lax.rsqrt(jnp.mean(d * d, axis=0, keepdims=True) + LN_EPS)
    o_ref[0] = (d.T * g_ref[...] * (1.0 - LAM_INIT)).astype(o_ref.dtype)


def _diff_attn(q, k, v, km, vm, slopes, lq1, lk1, lq2, lk2, g, *, tq, tk):
    b, s, _ = q.shape
    cols = 2 * tq
    assert tq == 2 * tk, "the pipeline below walks key blocks in pairs"
    small = lambda shape: pl.BlockSpec(shape, lambda bi, h, i: (0,) * len(shape))
    slot = lambda shape, dtype: [pltpu.VMEM(shape, dtype), pltpu.VMEM(shape, dtype)]
    return pl.pallas_call(
        functools.partial(_attn_kernel, tq=tq, tk=tk),
        out_shape=jax.ShapeDtypeStruct((b, s, N_HEADS * V_DIM), BF16),
        grid=(b, N_HEADS, s // tq),
        in_specs=[
            pl.BlockSpec((1, tq, V_DIM), lambda bi, h, i: (bi, i, h)),
            pl.BlockSpec((1, s, V_DIM), lambda bi, h, i: (bi, 0, h)),
            pl.BlockSpec((1, s, V_DIM), lambda bi, h, i: (bi, 0, h)),
            pl.BlockSpec((N_META, V_DIM), lambda bi, h, i: (0, h)),
            pl.BlockSpec((N_META, V_DIM), lambda bi, h, i: (0, h)),
            pl.BlockSpec((1, 1, cols), lambda bi, h, i: (h, 0, 0)),
            small((1, QK_DIM)), small((1, QK_DIM)), small((1, QK_DIM)), small((1, QK_DIM)),
            small((1, V_DIM)),
        ],
        out_specs=pl.BlockSpec((1, tq, V_DIM), lambda bi, h, i: (bi, i, h)),
        scratch_shapes=[
            pltpu.VMEM((V_DIM, cols), BF16),
            pltpu.VMEM((s // tk, V_DIM + ONES_ROWS, tk), BF16),
            pltpu.VMEM((1, cols), F32),
            pltpu.VMEM((V_DIM + ONES_ROWS, cols), F32),
            pltpu.VMEM((1 + tq // tk, tk, cols), F32),
            *slot((tk, cols), F32),
            *slot((1, cols), F32),
            *slot((tk, cols), BF16),
            *slot((1, cols), F32),
        ],
        compiler_params=pltpu.CompilerParams(
            dimension_semantics=("arbitrary", "arbitrary", "arbitrary"), vmem_limit_bytes=VMEM_LIMIT),
        name="diff_attn",
    )(q, k, v, km, vm, slopes, lq1, lk1, lq2, lk2, g)


def _mix_ln_kernel(a_ref, u_ref, halo_ref, um_ref, h_ref, wp_ref, ps_ref, wo_ref, g_ref, b_ref,
                   o_ref, ext_ref, p_ref, *, tm):
    i = pl.program_id(1)
    hw = N_META

    @pl.when(i == 0)
    def _():
        ext_ref[0:hw, :] = um_ref[...]

    @pl.when(i > 0)
    def _():
        ext_ref[0:hw, :] = halo_ref[0]

    ext_ref[hw:hw + tm, :] = u_ref[0]

    for gi, w in enumerate(POOL_WINDOWS):
        cols = slice(gi * POOL_GROUP, (gi + 1) * POOL_GROUP)
        cur = ext_ref[hw:hw + tm, cols]
        win = cur
        for back in range(1, w):
            win = win + ext_ref[hw - back:hw - back + tm, cols]
        pooled = win * (1.0 / w) - cur
        y = jnp.dot(pooled.astype(BF16), wp_ref[gi], preferred_element_type=F32)
        p_ref[:, cols] = (y * ps_ref[:, cols]).astype(BF16)

    aw = a_ref.shape[2]
    mix = jnp.dot(a_ref[0], wo_ref[0:aw, :], preferred_element_type=F32)
    mix = mix + jnp.dot(p_ref[...], wo_ref[aw:, :], preferred_element_type=F32)
    o_ref[0] = _layer_norm(ALPHA * h_ref[0] + mix, g_ref[...], b_ref[...])


def _mix_ln(a, u, um, h, wp, ps, wo, g, b, *, tm):
    bsz, s, d = h.shape
    aw = a.shape[2]
    pw = u.shape[2]
    hb = tm // N_META
    const2 = lambda shape: pl.BlockSpec(shape, lambda bi, i: (0,) * len(shape), pipeline_mode=pl.Buffered(1))
    return pl.pallas_call(
        functools.partial(_mix_ln_kernel, tm=tm),
        out_shape=jax.ShapeDtypeStruct((bsz, s, d), F32),
        grid=(bsz, s // tm),
        in_specs=[
            pl.BlockSpec((1, tm, aw), lambda bi, i: (bi, i, 0)),
            pl.BlockSpec((1, tm, pw), lambda bi, i: (bi, i, 0)),
            pl.BlockSpec((1, N_META, pw), lambda bi, i: (bi, jnp.maximum(i * hb - 1, 0), 0)),
            const2((N_META, pw)),
            pl.BlockSpec((1, tm, d), lambda bi, i: (bi, i, 0)),
            const2(wp.shape),
            const2((1, pw)),
            const2(wo.shape),
            const2((1, d)),
            const2((1, d)),
        ],
        out_specs=pl.BlockSpec((1, tm, d), lambda bi, i: (bi, i, 0)),
        scratch_shapes=[pltpu.VMEM((tm + N_META, pw), F32), pltpu.VMEM((tm, pw), BF16)],
        compiler_params=pltpu.CompilerParams(
            dimension_semantics=("arbitrary", "arbitrary"), vmem_limit_bytes=VMEM_LIMIT),
        name="mix_ln",
    )(a, u, u, um, h, wp, ps, wo, g, b)


def kernel(x, meta_tokens, ln1_g, ln1_b, ffn1_w_gate, ffn1_w_up, ffn1_w_down, w_in, lambda_q1, lambda_k1,
           lambda_q2, lambda_k2, subln_g, w_pool, pool_scale, w_out, ln2_g, ln2_b, ffn2_w_gate, ffn2_w_up,
           ffn2_w_down, ln3_g, ln3_b):
    bsz, s, d = x.shape
    l = 0
    tm, tf, tq, tk = 512, 512, 512, 256

    wg1, wu1, wd1 = (w[l].astype(BF16) for w in (ffn1_w_gate, ffn1_w_up, ffn1_w_down))
    wg2, wu2, wd2 = (w[l].astype(BF16) for w in (ffn2_w_gate, ffn2_w_up, ffn2_w_down))
    w_in_b = w_in[l].astype(BF16)
    w_pool_b = w_pool[l].astype(BF16)
    w_out_b = w_out[l].astype(BF16)
    row = lambda p: p[l].reshape(1, -1).astype(F32)

    h1, h1b = _ffn_ln(x.reshape(bsz * s, d), wg1, wu1, wd1, row(ln1_g), row(ln1_b), tm=tm, tf=tf)
    _, m1b = _ffn_ln(meta_tokens.astype(F32), wg1, wu1, wd1, row(ln1_g), row(ln1_b), tm=N_META, tf=tf)
    q, k, v, u = _in_proj(h1b, w_in_b, tm=tm)
    _, km, vm, um = _in_proj(m1b, w_in_b, tm=N_META)

    aw = N_HEADS * V_DIM
    slopes = 2.0 ** (-8.0 * jnp.arange(1, N_HEADS + 1, dtype=F32) / N_HEADS)
    slopes = jnp.broadcast_to(slopes[:, None, None], (N_HEADS, 1, 2 * tq))
    a_out = _diff_attn(q.reshape(bsz, s, aw), k.reshape(bsz, s, aw), v.reshape(bsz, s, aw), km, vm, slopes,
                       row(lambda_q1), row(lambda_k1), row(lambda_q2), row(lambda_k2), row(subln_g), tq=tq, tk=tk)

    h2 = _mix_ln(a_out, u.reshape(bsz, s, -1), um, h1.reshape(bsz, s, d), w_pool_b, row(pool_scale), w_out_b,
                 row(ln2_g), row(ln2_b), tm=tm)

    out, _ = _ffn_ln(h2.reshape(bsz * s, d), wg2, wu2, wd2, row(ln3_g), row(ln3_b), tm=tm, tf=tf)
    return out.reshape(bsz, s, d)
```

```python
import functools
import math

import jax
import jax.numpy as jnp
from jax import lax
from jax.experimental import pallas as pl
from jax.experimental.pallas import tpu as pltpu

F32 = jnp.float32
BF16 = jnp.bfloat16

CHUNK = 64
N_META = 16
N_HEADS = 8
QK_DIM = 64
V_DIM = 128
POOL_WINDOWS = (2, 4, 8, 16)
POOL_GROUP = 256
LN_EPS = 1e-5
DEPTH = 1
ALPHA = (2.0 * DEPTH) ** 0.25
LAM_INIT = 0.8 - 0.6 * math.exp(-0.3 * 0)
NEG_INF = -1e30
STRIP = 256
ONES_ROWS = 16

VMEM_LIMIT = 56 * 1024 * 1024


def _layer_norm(y, g, b):
    mu = jnp.mean(y, axis=-1, keepdims=True)
    yc = y - mu
    var = jnp.mean(yc * yc, axis=-1, keepdims=True)
    return yc * lax.rsqrt(var + LN_EPS) * g + b


def _ffn_ln_kernel(h_ref, wg_ref, wu_ref, wd_ref, g_ref, b_ref, o_ref, ob_ref, hb_ref, acc_ref):
    f = pl.program_id(1)
    nf = pl.num_programs(1)

    @pl.when(f == 0)
    def _():
        hb_ref[...] = h_ref[...].astype(BF16)
        acc_ref[...] = jnp.zeros_like(acc_ref)

    hb = hb_ref[...]
    gate = jnp.dot(hb, wg_ref[...], preferred_element_type=F32)
    up = jnp.dot(hb, wu_ref[...], preferred_element_type=F32)
    act = (gate / (1.0 + jnp.exp(-gate)) * up).astype(BF16)
    acc_ref[...] += jnp.dot(act, wd_ref[...], preferred_element_type=F32)

    @pl.when(f == nf - 1)
    def _():
        y = ALPHA * h_ref[...] + 0.5 * acc_ref[...]
        out = _layer_norm(y, g_ref[...], b_ref[...])
        o_ref[...] = out
        ob_ref[...] = out.astype(BF16)


def _ffn_ln(h, wg, wu, wd, g, b, *, tm, tf):
    n, d = h.shape
    ff = wg.shape[1]
    grid = (n // tm, ff // tf)
    return pl.pallas_call(
        _ffn_ln_kernel,
        out_shape=(jax.ShapeDtypeStruct((n, d), F32), jax.ShapeDtypeStruct((n, d), BF16)),
        grid=grid,
        in_specs=[
            pl.BlockSpec((tm, d), lambda i, f: (i, 0)),
            pl.BlockSpec((d, tf), lambda i, f: (0, f)),
            pl.BlockSpec((d, tf), lambda i, f: (0, f)),
            pl.BlockSpec((tf, d), lambda i, f: (f, 0)),
            pl.BlockSpec((1, d), lambda i, f: (0, 0)),
            pl.BlockSpec((1, d), lambda i, f: (0, 0)),
        ],
        out_specs=(
            pl.BlockSpec((tm, d), lambda i, f: (i, 0)),
            pl.BlockSpec((tm, d), lambda i, f: (i, 0)),
        ),
        scratch_shapes=[pltpu.VMEM((tm, d), BF16), pltpu.VMEM((tm, d), F32)],
        compiler_params=pltpu.CompilerParams(
            dimension_semantics=("arbitrary", "arbitrary"), vmem_limit_bytes=VMEM_LIMIT),
        name="ffn_ln",
    )(h, wg, wu, wd, g, b)


def _in_proj_kernel(x_ref, w_ref, q_ref, k_ref, v_ref, u_ref):
    x = x_ref[...]
    aw = N_HEADS * V_DIM
    pair = 2 * V_DIM
    for out_ref, base, scale in ((q_ref, 0, QK_DIM ** -0.5), (k_ref, aw, None), (v_ref, 2 * aw, None)):
        for hp in range(N_HEADS // 2):
            res = jnp.dot(x, w_ref[:, base + hp * pair:base + (hp + 1) * pair], preferred_element_type=F32)
            if scale is not None:
                res = res * scale
            out_ref[2 * hp] = res[:, 0:V_DIM].astype(BF16)
            out_ref[2 * hp + 1] = res[:, V_DIM:pair].astype(BF16)
    u_ref[...] = jnp.dot(x, w_ref[:, 3 * aw:], preferred_element_type=F32)


def _in_proj(x, w, *, tm):
    n, d = x.shape
    aw = N_HEADS * V_DIM
    pw = w.shape[1] - 3 * aw
    row = lambda i: (i, 0)
    heads = lambda i: (0, i, 0)
    return pl.pallas_call(
        _in_proj_kernel,
        out_shape=(
            jax.ShapeDtypeStruct((N_HEADS, n, V_DIM), BF16),
            jax.ShapeDtypeStruct((N_HEADS, n, V_DIM), BF16),
            jax.ShapeDtypeStruct((N_HEADS, n, V_DIM), BF16),
            jax.ShapeDtypeStruct((n, pw), F32),
        ),
        grid=(n // tm,),
        in_specs=[
            pl.BlockSpec((tm, d), row),
            pl.BlockSpec(w.shape, lambda i: (0, 0), pipeline_mode=pl.Buffered(1)),
        ],
        out_specs=(
            pl.BlockSpec((N_HEADS, tm, V_DIM), heads),
            pl.BlockSpec((N_HEADS, tm, V_DIM), heads),
            pl.BlockSpec((N_HEADS, tm, V_DIM), heads),
            pl.BlockSpec((tm, pw), row),
        ),
        compiler_params=pltpu.CompilerParams(
            dimension_semantics=("arbitrary",), vmem_limit_bytes=VMEM_LIMIT),
        name="in_proj",
    )(x, w)


def _attn_kernel(q_ref, k_ref, v_ref, km_ref, vm_ref, slope_ref, lq1_ref, lk1_ref, lq2_ref, lk2_ref,
                 g_ref, o_ref, qs_ref, vt_ref, m_ref, acc_ref, bias_ref,
                 s0_ref, s1_ref, c0_ref, c1_ref, p0_ref, p1_ref, a0_ref, a1_ref, *, tq, tk):
    qi = pl.program_id(2)
    cols = 2 * tq
    nd = tq // tk
    n_off = nd * qi
    slope = slope_ref[0]
    tn = (((0,), (0,)), ((), ()))
    s_refs, c_refs, p_refs, a_refs = (s0_ref, s1_ref), (c0_ref, c1_ref), (p0_ref, p1_ref), (a0_ref, a1_ref)

    def query_index(shape):
        c = lax.broadcasted_iota(jnp.int32, shape, 1)
        return jnp.where(c >= tq, c - tq, c)

    @pl.when(qi == 0)
    def _():
        i = query_index((tk, cols))
        j = lax.broadcasted_iota(jnp.int32, (tk, cols), 0)
        bias_ref[0] = -slope * (i - j).astype(F32)
        for t in range(nd):
            jt = j + t * tk
            visible = (i // CHUNK) >= (jt // CHUNK)
            bias_ref[1 + t] = jnp.where(visible, -slope * jnp.abs(i - jt).astype(F32), NEG_INF)

        def transpose_values(c, carry):
            start = pl.multiple_of(c * tk, tk)
            vt_ref[c, 0:V_DIM, :] = v_ref[0, pl.ds(start, tk), :].astype(F32).T.astype(BF16)
            vt_ref[c, V_DIM:V_DIM + ONES_ROWS, :] = jnp.ones((ONES_ROWS, tk), BF16)
            return carry

        lax.fori_loop(0, v_ref.shape[1] // tk, transpose_values, 0)

    qt = q_ref[0].astype(F32).T
    row = lax.broadcasted_iota(jnp.int32, qt.shape, 0)
    qs_ref[:, 0:tq] = jnp.where(row < QK_DIM, qt, 0.0).astype(BF16)
    qs_ref[:, tq:cols] = jnp.where(row >= QK_DIM, qt, 0.0).astype(BF16)

    s = jnp.dot(km_ref[0], qs_ref[...], preferred_element_type=F32)
    i = query_index((N_META, cols))
    j = lax.broadcasted_iota(jnp.int32, (N_META, cols), 0)
    s = s - slope * (i - j + (N_META + qi * tq)).astype(F32)
    m0 = jnp.max(s, axis=0, keepdims=True)
    p = jnp.exp(s - m0)
    pb = p.astype(BF16)
    m_ref[...] = m0
    acc_ref[0:V_DIM, :] = lax.dot_general(vm_ref[0], pb, tn, preferred_element_type=F32)
    acc_ref[V_DIM:V_DIM + ONES_ROWS, :] = jnp.broadcast_to(
        jnp.sum(pb.astype(F32), axis=0, keepdims=True), (ONES_ROWS, cols))

    def scores(b, slot):
        start = pl.multiple_of(b * tk, tk)
        table = jnp.where(b < n_off, 0, b - n_off + 1)
        s = jnp.dot(k_ref[0, pl.ds(start, tk), :], qs_ref[...], preferred_element_type=F32)
        s = s + bias_ref[table]
        s_refs[slot][...] = s
        c_refs[slot][...] = jnp.max(s, axis=0, keepdims=True)

    def softmax(b, slot):
        gap = jnp.where(b < n_off, qi * tq - b * tk, 0)
        shift = -slope * jnp.full((1, cols), gap, jnp.int32).astype(F32)
        m_prev = m_ref[...]
        m_new = jnp.maximum(m_prev, c_refs[slot][...] + shift)
        offset = m_new - shift
        for c in range(0, cols, STRIP):
            p = jnp.exp(s_refs[slot][:, c:c + STRIP] - offset[:, c:c + STRIP])
            p_refs[slot][:, c:c + STRIP] = p.astype(BF16)
        a_refs[slot][...] = jnp.exp(m_prev - m_new)
        m_ref[...] = m_new

    def values(b, slot):
        pv = jnp.dot(vt_ref[b], p_refs[slot][...], preferred_element_type=F32)
        acc_ref[...] = a_refs[slot][...] * acc_ref[...] + pv

    scores(0, 0)
    scores(1, 1)
    softmax(0, 0)

    def pair(t, carry):
        b = 2 * t
        softmax(b - 1, 1)
        values(b - 2, 0)
        scores(b, 0)
        softmax(b, 0)
        values(b - 1, 1)
        scores(b + 1, 1)
        return carry

    n_blocks = n_off + nd
    lax.fori_loop(1, n_blocks // 2, pair, 0)
    softmax(n_blocks - 1, 1)
    values(n_blocks - 2, 0)
    values(n_blocks - 1, 1)

    lam = (jnp.exp(jnp.sum(lq1_ref[...] * lk1_ref[...], axis=1, keepdims=True))
           - jnp.exp(jnp.sum(lq2_ref[...] * lk2_ref[...], axis=1, keepdims=True)) + LAM_INIT)
    o = acc_ref[0:V_DIM, :] / acc_ref[V_DIM:V_DIM + 1, :]
    d = o[:, 0:tq] - lam * o[:, tq:cols]
    d = d * lax.rsqrt(jnp.mean(d * d, axis=0, keepdims=True) + LN_EPS)
    o_ref[0] = (d.T * g_ref[...] * (1.0 - LAM_INIT)).astype(o_ref.dtype)


def _diff_attn(q, k, v, km, vm, slopes, lq1, lk1, lq2, lk2, g, *, b, tq, tk):
    s = q.shape[1] // b
    nq = s // tq
    cols = 2 * tq
    assert tq == 2 * tk, "the pipeline below walks key blocks in pairs"
    small = lambda shape: pl.BlockSpec(shape, lambda bi, h, i: (0,) * len(shape))
    slot = lambda shape, dtype: [pltpu.VMEM(shape, dtype), pltpu.VMEM(shape, dtype)]
    return pl.pallas_call(
        functools.partial(_attn_kernel, tq=tq, tk=tk),
        out_shape=jax.ShapeDtypeStruct((b, s, N_HEADS * V_DIM), BF16),
        grid=(b, N_HEADS, s // tq),
        in_specs=[
            pl.BlockSpec((1, tq, V_DIM), lambda bi, h, i: (h, bi * nq + i, 0)),
            pl.BlockSpec((1, s, V_DIM), lambda bi, h, i: (h, bi, 0)),
            pl.BlockSpec((1, s, V_DIM), lambda bi, h, i: (h, bi, 0)),
            pl.BlockSpec((1, N_META, V_DIM), lambda bi, h, i: (h, 0, 0)),
            pl.BlockSpec((1, N_META, V_DIM), lambda bi, h, i: (h, 0, 0)),
            pl.BlockSpec((1, 1, cols), lambda bi, h, i: (h, 0, 0)),
            small((1, QK_DIM)), small((1, QK_DIM)), small((1, QK_DIM)), small((1, QK_DIM)),
            small((1, V_DIM)),
        ],
        out_specs=pl.BlockSpec((1, tq, V_DIM), lambda bi, h, i: (bi, i, h)),
        scratch_shapes=[
            pltpu.VMEM((V_DIM, cols), BF16),
            pltpu.VMEM((s // tk, V_DIM + ONES_ROWS, tk), BF16),
            pltpu.VMEM((1, cols), F32),
            pltpu.VMEM((V_DIM + ONES_ROWS, cols), F32),
            pltpu.VMEM((1 + tq // tk, tk, cols), F32),
            *slot((tk, cols), F32),
            *slot((1, cols), F32),
            *slot((tk, cols), BF16),
            *slot((1, cols), F32),
        ],
        compiler_params=pltpu.CompilerParams(
            dimension_semantics=("arbitrary", "arbitrary", "arbitrary"), vmem_limit_bytes=VMEM_LIMIT),
        name="diff_attn",
    )(q, k, v, km, vm, slopes, lq1, lk1, lq2, lk2, g)


def _mix_ln_kernel(a_ref, u_ref, halo_ref, um_ref, h_ref, wp_ref, ps_ref, wo_ref, g_ref, b_ref,
                   o_ref, ext_ref, p_ref, *, tm):
    i = pl.program_id(1)
    hw = N_META

    @pl.when(i == 0)
    def _():
        ext_ref[0:hw, :] = um_ref[...]

    @pl.when(i > 0)
    def _():
        ext_ref[0:hw, :] = halo_ref[0]

    ext_ref[hw:hw + tm, :] = u_ref[0]

    for gi, w in enumerate(POOL_WINDOWS):
        cols = slice(gi * POOL_GROUP, (gi + 1) * POOL_GROUP)
        cur = ext_ref[hw:hw + tm, cols]
        win = cur
        for back in range(1, w):
            win = win + ext_ref[hw - back:hw - back + tm, cols]
        pooled = win * (1.0 / w) - cur
        y = jnp.dot(pooled.astype(BF16), wp_ref[gi], preferred_element_type=F32)
        p_ref[:, cols] = (y * ps_ref[:, cols]).astype(BF16)

    aw = a_ref.shape[2]
    mix = jnp.dot(a_ref[0], wo_ref[0:aw, :], preferred_element_type=F32)
    mix = mix + jnp.dot(p_ref[...], wo_ref[aw:, :], preferred_element_type=F32)
    o_ref[0] = _layer_norm(ALPHA * h_ref[0] + mix, g_ref[...], b_ref[...])


def _mix_ln(a, u, um, h, wp, ps, wo, g, b, *, tm):
    bsz, s, d = h.shape
    aw = a.shape[2]
    pw = u.shape[2]
    hb = tm // N_META
    const2 = lambda shape: pl.BlockSpec(shape, lambda bi, i: (0,) * len(shape), pipeline_mode=pl.Buffered(1))
    return pl.pallas_call(
        functools.partial(_mix_ln_kernel, tm=tm),
        out_shape=jax.ShapeDtypeStruct((bsz, s, d), F32),
        grid=(bsz, s // tm),
        in_specs=[
            pl.BlockSpec((1, tm, aw), lambda bi, i: (bi, i, 0)),
            pl.BlockSpec((1, tm, pw), lambda bi, i: (bi, i, 0)),
            pl.BlockSpec((1, N_META, pw), lambda bi, i: (bi, jnp.maximum(i * hb - 1, 0), 0)),
            const2((N_META, pw)),
            pl.BlockSpec((1, tm, d), lambda bi, i: (bi, i, 0)),
            const2(wp.shape),
            const2((1, pw)),
            const2(wo.shape),
            const2((1, d)),
            const2((1, d)),
        ],
        out_specs=pl.BlockSpec((1, tm, d), lambda bi, i: (bi, i, 0)),
        scratch_shapes=[pltpu.VMEM((tm + N_META, pw), F32), pltpu.VMEM((tm, pw), BF16)],
        compiler_params=pltpu.CompilerParams(
            dimension_semantics=("arbitrary", "arbitrary"), vmem_limit_bytes=VMEM_LIMIT),
        name="mix_ln",
    )(a, u, u, um, h, wp, ps, wo, g, b)


def kernel(x, meta_tokens, ln1_g, ln1_b, ffn1_w_gate, ffn1_w_up, ffn1_w_down, w_in, lambda_q1, lambda_k1,
           lambda_q2, lambda_k2, subln_g, w_pool, pool_scale, w_out, ln2_g, ln2_b, ffn2_w_gate, ffn2_w_up,
           ffn2_w_down, ln3_g, ln3_b):
    bsz, s, d = x.shape
    l = 0
    tm, tf, tq, tk = 512, 512, 512, 256

    wg1, wu1, wd1 = (w[l].astype(BF16) for w in (ffn1_w_gate, ffn1_w_up, ffn1_w_down))
    wg2, wu2, wd2 = (w[l].astype(BF16) for w in (ffn2_w_gate, ffn2_w_up, ffn2_w_down))
    w_in_b = w_in[l].astype(BF16)
    w_pool_b = w_pool[l].astype(BF16)
    w_out_b = w_out[l].astype(BF16)
    row = lambda p: p[l].reshape(1, -1).astype(F32)

    h1, h1b = _ffn_ln(x.reshape(bsz * s, d), wg1, wu1, wd1, row(ln1_g), row(ln1_b), tm=tm, tf=tf)
    _, m1b = _ffn_ln(meta_tokens.astype(F32), wg1, wu1, wd1, row(ln1_g), row(ln1_b), tm=N_META, tf=tf)
    q, k, v, u = _in_proj(h1b, w_in_b, tm=tm)
    _, km, vm, um = _in_proj(m1b, w_in_b, tm=N_META)

    slopes = 2.0 ** (-8.0 * jnp.arange(1, N_HEADS + 1, dtype=F32) / N_HEADS)
    slopes = jnp.broadcast_to(slopes[:, None, None], (N_HEADS, 1, 2 * tq))
    a_out = _diff_attn(q, k, v, km, vm, slopes, row(lambda_q1), row(lambda_k1), row(lambda_q2), row(lambda_k2),
                       row(subln_g), b=bsz, tq=tq, tk=tk)

    h2 = _mix_ln(a_out, u.reshape(bsz, s, -1), um, h1.reshape(bsz, s, d), w_pool_b, row(pool_scale), w_out_b,
                 row(ln2_g), row(ln2_b), tm=tm)

    out, _ = _ffn_ln(h2.reshape(bsz * s, d), wg2, wu2, wd2, row(ln3_g), row(ln3_b), tm=tm, tf=tf)
    return out.reshape(bsz, s, d)
```

```python
import functools
import math

import jax
import jax.numpy as jnp
from jax import lax
from jax.experimental import pallas as pl
from jax.experimental.pallas import tpu as pltpu

F32 = jnp.float32
BF16 = jnp.bfloat16

CHUNK = 64
N_META = 16
N_HEADS = 8
QK_DIM = 64
V_DIM = 128
POOL_WINDOWS = (2, 4, 8, 16)
POOL_GROUP = 256
LN_EPS = 1e-5
DEPTH = 1
ALPHA = (2.0 * DEPTH) ** 0.25
LAM_INIT = 0.8 - 0.6 * math.exp(-0.3 * 0)
NEG_INF = -1e30
LOG2E = math.log2(math.e)
STRIP = 256
ONES_ROWS = 16

VMEM_LIMIT = 56 * 1024 * 1024


def _layer_norm(y, g, b):
    mu = jnp.mean(y, axis=-1, keepdims=True)
    yc = y - mu
    var = jnp.mean(yc * yc, axis=-1, keepdims=True)
    return yc * lax.rsqrt(var + LN_EPS) * g + b


def _ffn_ln_kernel(h_ref, wg_ref, wu_ref, wd_ref, g_ref, b_ref, o_ref, ob_ref, hb_ref, acc_ref):
    f = pl.program_id(1)
    nf = pl.num_programs(1)

    @pl.when(f == 0)
    def _():
        hb_ref[...] = h_ref[...].astype(BF16)
        acc_ref[...] = jnp.zeros_like(acc_ref)

    hb = hb_ref[...]
    gate = jnp.dot(hb, wg_ref[...], preferred_element_type=F32)
    up = jnp.dot(hb, wu_ref[...], preferred_element_type=F32)
    act = (gate / (1.0 + jnp.exp(-gate)) * up).astype(BF16)
    acc_ref[...] += jnp.dot(act, wd_ref[...], preferred_element_type=F32)

    @pl.when(f == nf - 1)
    def _():
        y = ALPHA * h_ref[...] + 0.5 * acc_ref[...]
        out = _layer_norm(y, g_ref[...], b_ref[...])
        o_ref[...] = out
        ob_ref[...] = out.astype(BF16)


def _ffn_ln(h, wg, wu, wd, g, b, *, tm, tf):
    n, d = h.shape
    ff = wg.shape[1]
    grid = (n // tm, ff // tf)
    return pl.pallas_call(
        _ffn_ln_kernel,
        out_shape=(jax.ShapeDtypeStruct((n, d), F32), jax.ShapeDtypeStruct((n, d), BF16)),
        grid=grid,
        in_specs=[
            pl.BlockSpec((tm, d), lambda i, f: (i, 0)),
            pl.BlockSpec((d, tf), lambda i, f: (0, f)),
            pl.BlockSpec((d, tf), lambda i, f: (0, f)),
            pl.BlockSpec((tf, d), lambda i, f: (f, 0)),
            pl.BlockSpec((1, d), lambda i, f: (0, 0)),
            pl.BlockSpec((1, d), lambda i, f: (0, 0)),
        ],
        out_specs=(
            pl.BlockSpec((tm, d), lambda i, f: (i, 0)),
            pl.BlockSpec((tm, d), lambda i, f: (i, 0)),
        ),
        scratch_shapes=[pltpu.VMEM((tm, d), BF16), pltpu.VMEM((tm, d), F32)],
        compiler_params=pltpu.CompilerParams(
            dimension_semantics=("arbitrary", "arbitrary"), vmem_limit_bytes=VMEM_LIMIT),
        name="ffn_ln",
    )(h, wg, wu, wd, g, b)


def _in_proj_kernel(x_ref, w_ref, q_ref, k_ref, v_ref, u_ref):
    x = x_ref[...]
    aw = N_HEADS * V_DIM
    pair = 2 * V_DIM
    for out_ref, base, scale in ((q_ref, 0, LOG2E * QK_DIM ** -0.5), (k_ref, aw, None), (v_ref, 2 * aw, None)):
        for hp in range(N_HEADS // 2):
            res = jnp.dot(x, w_ref[:, base + hp * pair:base + (hp + 1) * pair], preferred_element_type=F32)
            if scale is not None:
                res = res * scale
            out_ref[2 * hp] = res[:, 0:V_DIM].astype(BF16)
            out_ref[2 * hp + 1] = res[:, V_DIM:pair].astype(BF16)
    u_ref[...] = jnp.dot(x, w_ref[:, 3 * aw:], preferred_element_type=F32)


def _in_proj(x, w, *, tm):
    n, d = x.shape
    aw = N_HEADS * V_DIM
    pw = w.shape[1] - 3 * aw
    row = lambda i: (i, 0)
    heads = lambda i: (0, i, 0)
    return pl.pallas_call(
        _in_proj_kernel,
        out_shape=(
            jax.ShapeDtypeStruct((N_HEADS, n, V_DIM), BF16),
            jax.ShapeDtypeStruct((N_HEADS, n, V_DIM), BF16),
            jax.ShapeDtypeStruct((N_HEADS, n, V_DIM), BF16),
            jax.ShapeDtypeStruct((n, pw), F32),
        ),
        grid=(n // tm,),
        in_specs=[
            pl.BlockSpec((tm, d), row),
            pl.BlockSpec(w.shape, lambda i: (0, 0), pipeline_mode=pl.Buffered(1)),
        ],
        out_specs=(
            pl.BlockSpec((N_HEADS, tm, V_DIM), heads),
            pl.BlockSpec((N_HEADS, tm, V_DIM), heads),
            pl.BlockSpec((N_HEADS, tm, V_DIM), heads),
            pl.BlockSpec((tm, pw), row),
        ),
        compiler_params=pltpu.CompilerParams(
            dimension_semantics=("arbitrary",), vmem_limit_bytes=VMEM_LIMIT),
        name="in_proj",
    )(x, w)


def _attn_kernel(q_ref, k_ref, v_ref, km_ref, vm_ref, slope_ref, lq1_ref, lk1_ref, lq2_ref, lk2_ref,
                 g_ref, o_ref, qs_ref, vt_ref, m_ref, acc_ref, bias_ref,
                 s0_ref, s1_ref, c0_ref, c1_ref, p0_ref, p1_ref, a0_ref, a1_ref, *, tq, tk):
    qi = pl.program_id(2)
    cols = 2 * tq
    nd = tq // tk
    n_off = nd * qi
    slope = slope_ref[0] * LOG2E
    tn = (((0,), (0,)), ((), ()))
    s_refs, c_refs, p_refs, a_refs = (s0_ref, s1_ref), (c0_ref, c1_ref), (p0_ref, p1_ref), (a0_ref, a1_ref)

    def query_index(shape):
        c = lax.broadcasted_iota(jnp.int32, shape, 1)
        return jnp.where(c >= tq, c - tq, c)

    @pl.when(qi == 0)
    def _():
        i = query_index((tk, cols))
        j = lax.broadcasted_iota(jnp.int32, (tk, cols), 0)
        bias_ref[0] = -slope * (i - j).astype(F32)
        for t in range(nd):
            jt = j + t * tk
            visible = (i // CHUNK) >= (jt // CHUNK)
            bias_ref[1 + t] = jnp.where(visible, -slope * jnp.abs(i - jt).astype(F32), NEG_INF)

        def transpose_values(c, carry):
            start = pl.multiple_of(c * tk, tk)
            vt_ref[c, 0:V_DIM, :] = v_ref[0, pl.ds(start, tk), :].astype(F32).T.astype(BF16)
            vt_ref[c, V_DIM:V_DIM + ONES_ROWS, :] = jnp.ones((ONES_ROWS, tk), BF16)
            return carry

        lax.fori_loop(0, v_ref.shape[1] // tk, transpose_values, 0)

    qt = q_ref[0].astype(F32).T
    row = lax.broadcasted_iota(jnp.int32, qt.shape, 0)
    qs_ref[:, 0:tq] = jnp.where(row < QK_DIM, qt, 0.0).astype(BF16)
    qs_ref[:, tq:cols] = jnp.where(row >= QK_DIM, qt, 0.0).astype(BF16)

    s = jnp.dot(km_ref[0], qs_ref[...], preferred_element_type=F32)
    i = query_index((N_META, cols))
    j = lax.broadcasted_iota(jnp.int32, (N_META, cols), 0)
    s = s - slope * (i - j + (N_META + qi * tq)).astype(F32)
    m0 = jnp.max(s, axis=0, keepdims=True)
    p = jnp.exp2(s - m0)
    pb = p.astype(BF16)
    m_ref[...] = m0
    acc_ref[0:V_DIM, :] = lax.dot_general(vm_ref[0], pb, tn, preferred_element_type=F32)
    acc_ref[V_DIM:V_DIM + ONES_ROWS, :] = jnp.broadcast_to(
        jnp.sum(pb.astype(F32), axis=0, keepdims=True), (ONES_ROWS, cols))

    def scores(b, slot):
        start = pl.multiple_of(b * tk, tk)
        table = jnp.where(b < n_off, 0, b - n_off + 1)
        s = jnp.dot(k_ref[0, pl.ds(start, tk), :], qs_ref[...], preferred_element_type=F32)
        s = s + bias_ref[table]
        s_refs[slot][...] = s
        c_refs[slot][...] = jnp.max(s, axis=0, keepdims=True)

    def softmax(b, slot):
        gap = jnp.where(b < n_off, qi * tq - b * tk, 0)
        shift = -slope * jnp.full((1, cols), gap, jnp.int32).astype(F32)
        m_prev = m_ref[...]
        m_new = jnp.maximum(m_prev, c_refs[slot][...] + shift)
        offset = m_new - shift
        for c in range(0, cols, STRIP):
            p = jnp.exp2(s_refs[slot][:, c:c + STRIP] - offset[:, c:c + STRIP])
            p_refs[slot][:, c:c + STRIP] = p.astype(BF16)
        a_refs[slot][...] = jnp.exp2(m_prev - m_new)
        m_ref[...] = m_new

    def values(b, slot):
        pv = jnp.dot(vt_ref[b], p_refs[slot][...], preferred_element_type=F32)
        acc_ref[...] = a_refs[slot][...] * acc_ref[...] + pv

    scores(0, 0)
    scores(1, 1)
    softmax(0, 0)

    def pair(t, carry):
        b = 2 * t
        softmax(b - 1, 1)
        values(b - 2, 0)
        scores(b, 0)
        softmax(b, 0)
        values(b - 1, 1)
        scores(b + 1, 1)
        return carry

    n_blocks = n_off + nd
    lax.fori_loop(1, n_blocks // 2, pair, 0)
    softmax(n_blocks - 1, 1)
    values(n_blocks - 2, 0)
    values(n_blocks - 1, 1)

    lam = (jnp.exp(jnp.sum(lq1_ref[...] * lk1_ref[...], axis=1, keepdims=True))
           - jnp.exp(jnp.sum(lq2_ref[...] * lk2_ref[...], axis=1, keepdims=True)) + LAM_INIT)
    o = acc_ref[0:V_DIM, :] / acc_ref[V_DIM:V_DIM + 1, :]
    d = o[:, 0:tq] - lam * o[:, tq:cols]
    d = d * lax.rsqrt(jnp.mean(d * d, axis=0, keepdims=True) + LN_EPS)
    o_ref[0] = (d.T * g_ref[...] * (1.0 - LAM_INIT)).astype(o_ref.dtype)


def _diff_attn(q, k, v, km, vm, slopes, lq1, lk1, lq2, lk2, g, *, b, tq, tk):
    s = q.shape[1] // b
    nq = s // tq
    cols = 2 * tq
    assert tq == 2 * tk, "the pipeline below walks key blocks in pairs"
    small = lambda shape: pl.BlockSpec(shape, lambda bi, h, i: (0,) * len(shape))
    slot = lambda shape, dtype: [pltpu.VMEM(shape, dtype), pltpu.VMEM(shape, dtype)]
    return pl.pallas_call(
        functools.partial(_attn_kernel, tq=tq, tk=tk),
        out_shape=jax.ShapeDtypeStruct((b, s, N_HEADS * V_DIM), BF16),
        grid=(b, N_HEADS, s // tq),
        in_specs=[
            pl.BlockSpec((1, tq, V_DIM), lambda bi, h, i: (h, bi * nq + i, 0)),
            pl.BlockSpec((1, s, V_DIM), lambda bi, h, i: (h, bi, 0)),
            pl.BlockSpec((1, s, V_DIM), lambda bi, h, i: (h, bi, 0)),
            pl.BlockSpec((1, N_META, V_DIM), lambda bi, h, i: (h, 0, 0)),
            pl.BlockSpec((1, N_META, V_DIM), lambda bi, h, i: (h, 0, 0)),
            pl.BlockSpec((1, 1, cols), lambda bi, h, i: (h, 0, 0)),
            small((1, QK_DIM)), small((1, QK_DIM)), small((1, QK_DIM)), small((1, QK_DIM)),
            small((1, V_DIM)),
        ],
        out_specs=pl.BlockSpec((1, tq, V_DIM), lambda bi, h, i: (bi, i, h)),
        scratch_shapes=[
            pltpu.VMEM((V_DIM, cols), BF16),
            pltpu.VMEM((s // tk, V_DIM + ONES_ROWS, tk), BF16),
            pltpu.VMEM((1, cols), F32),
            pltpu.VMEM((V_DIM + ONES_ROWS, cols), F32),
            pltpu.VMEM((1 + tq // tk, tk, cols), F32),
            *slot((tk, cols), F32),
            *slot((1, cols), F32),
            *slot((tk, cols), BF16),
            *slot((1, cols), F32),
        ],
        compiler_params=pltpu.CompilerParams(
            dimension_semantics=("arbitrary", "arbitrary", "arbitrary"), vmem_limit_bytes=VMEM_LIMIT),
        name="diff_attn",
    )(q, k, v, km, vm, slopes, lq1, lk1, lq2, lk2, g)


def _mix_ln_kernel(a_ref, u_ref, halo_ref, um_ref, h_ref, wp_ref, ps_ref, wo_ref, g_ref, b_ref,
                   o_ref, ext_ref, p_ref, *, tm):
    i = pl.program_id(1)
    hw = N_META

    @pl.when(i == 0)
    def _():
        ext_ref[0:hw, :] = um_ref[...]

    @pl.when(i > 0)
    def _():
        ext_ref[0:hw, :] = halo_ref[0]

    ext_ref[hw:hw + tm, :] = u_ref[0]

    for gi, w in enumerate(POOL_WINDOWS):
        cols = slice(gi * POOL_GROUP, (gi + 1) * POOL_GROUP)
        cur = ext_ref[hw:hw + tm, cols]
        win = cur
        for back in range(1, w):
            win = win + ext_ref[hw - back:hw - back + tm, cols]
        pooled = win * (1.0 / w) - cur
        y = jnp.dot(pooled.astype(BF16), wp_ref[gi], preferred_element_type=F32)
        p_ref[:, cols] = (y * ps_ref[:, cols]).astype(BF16)

    aw = a_ref.shape[2]
    mix = jnp.dot(a_ref[0], wo_ref[0:aw, :], preferred_element_type=F32)
    mix = mix + jnp.dot(p_ref[...], wo_ref[aw:, :], preferred_element_type=F32)
    o_ref[0] = _layer_norm(ALPHA * h_ref[0] + mix, g_ref[...], b_ref[...])


def _mix_ln(a, u, um, h, wp, ps, wo, g, b, *, tm):
    bsz, s, d = h.shape
    aw = a.shape[2]
    pw = u.shape[2]
    hb = tm // N_META
    const2 = lambda shape: pl.BlockSpec(shape, lambda bi, i: (0,) * len(shape), pipeline_mode=pl.Buffered(1))
    return pl.pallas_call(
        functools.partial(_mix_ln_kernel, tm=tm),
        out_shape=jax.ShapeDtypeStruct((bsz, s, d), F32),
        grid=(bsz, s // tm),
        in_specs=[
            pl.BlockSpec((1, tm, aw), lambda bi, i: (bi, i, 0)),
            pl.BlockSpec((1, tm, pw), lambda bi, i: (bi, i, 0)),
            pl.BlockSpec((1, N_META, pw), lambda bi, i: (bi, jnp.maximum(i * hb - 1, 0), 0)),
            const2((N_META, pw)),
            pl.BlockSpec((1, tm, d), lambda bi, i: (bi, i, 0)),
            const2(wp.shape),
            const2((1, pw)),
            const2(wo.shape),
            const2((1, d)),
            const2((1, d)),
        ],
        out_specs=pl.BlockSpec((1, tm, d), lambda bi, i: (bi, i, 0)),
        scratch_shapes=[pltpu.VMEM((tm + N_META, pw), F32), pltpu.VMEM((tm, pw), BF16)],
        compiler_params=pltpu.CompilerParams(
            dimension_semantics=("arbitrary", "arbitrary"), vmem_limit_bytes=VMEM_LIMIT),
        name="mix_ln",
    )(a, u, u, um, h, wp, ps, wo, g, b)


def kernel(x, meta_tokens, ln1_g, ln1_b, ffn1_w_gate, ffn1_w_up, ffn1_w_down, w_in, lambda_q1, lambda_k1,
           lambda_q2, lambda_k2, subln_g, w_pool, pool_scale, w_out, ln2_g, ln2_b, ffn2_w_gate, ffn2_w_up,
           ffn2_w_down, ln3_g, ln3_b):
    bsz, s, d = x.shape
    l = 0
    tm, tf, tq, tk = 512, 512, 512, 256

    wg1, wu1, wd1 = (w[l].astype(BF16) for w in (ffn1_w_gate, ffn1_w_up, ffn1_w_down))
    wg2, wu2, wd2 = (w[l].astype(BF16) for w in (ffn2_w_gate, ffn2_w_up, ffn2_w_down))
    w_in_b = w_in[l].astype(BF16)
    w_pool_b = w_pool[l].astype(BF16)
    w_out_b = w_out[l].astype(BF16)
    row = lambda p: p[l].reshape(1, -1).astype(F32)

    h1, h1b = _ffn_ln(x.reshape(bsz * s, d), wg1, wu1, wd1, row(ln1_g), row(ln1_b), tm=tm, tf=tf)
    _, m1b = _ffn_ln(meta_tokens.astype(F32), wg1, wu1, wd1, row(ln1_g), row(ln1_b), tm=N_META, tf=tf)
    q, k, v, u = _in_proj(h1b, w_in_b, tm=tm)
    _, km, vm, um = _in_proj(m1b, w_in_b, tm=N_META)

    slopes = 2.0 ** (-8.0 * jnp.arange(1, N_HEADS + 1, dtype=F32) / N_HEADS)
    slopes = jnp.broadcast_to(slopes[:, None, None], (N_HEADS, 1, 2 * tq))
    a_out = _diff_attn(q, k, v, km, vm, slopes, row(lambda_q1), row(lambda_k1), row(lambda_q2), row(lambda_k2),
                       row(subln_g), b=bsz, tq=tq, tk=tk)

    h2 = _mix_ln(a_out, u.reshape(bsz, s, -1), um, h1.reshape(bsz, s, d), w_pool_b, row(pool_scale), w_out_b,
                 row(ln2_g), row(ln2_b), tm=tm)

    out, _ = _ffn_ln(h2.reshape(bsz * s, d), wg2, wu2, wd2, row(ln3_g), row(ln3_b), tm=tm, tf=tf)
    return out.reshape(bsz, s, d)
```

```python
import functools
import math

import jax
import jax.numpy as jnp
from jax import lax
from jax.experimental import pallas as pl
from jax.experimental.pallas import tpu as pltpu

F32 = jnp.float32
BF16 = jnp.bfloat16

CHUNK = 64
N_META = 16
N_HEADS = 8
QK_DIM = 64
V_DIM = 128
POOL_WINDOWS = (2, 4, 8, 16)
POOL_GROUP = 256
LN_EPS = 1e-5
DEPTH = 1
ALPHA = (2.0 * DEPTH) ** 0.25
LAM_INIT = 0.8 - 0.6 * math.exp(-0.3 * 0)
NEG_INF = -1e30
LOG2E = math.log2(math.e)
STRIP = 256
ONES_ROWS = 16

VMEM_LIMIT = 56 * 1024 * 1024


def _layer_norm(y, g, b):
    mu = jnp.mean(y, axis=-1, keepdims=True)
    yc = y - mu
    var = jnp.mean(yc * yc, axis=-1, keepdims=True)
    return yc * lax.rsqrt(var + LN_EPS) * g + b


def _ffn_ln_kernel(h_ref, wg_ref, wu_ref, wd_ref, g_ref, b_ref, o_ref, ob_ref, hb_ref, acc_ref):
    f = pl.program_id(1)
    nf = pl.num_programs(1)

    @pl.when(f == 0)
    def _():
        hb_ref[...] = h_ref[...].astype(BF16)
        acc_ref[...] = jnp.zeros_like(acc_ref)

    hb = hb_ref[...]
    gate = jnp.dot(hb, wg_ref[...], preferred_element_type=F32)
    up = jnp.dot(hb, wu_ref[...], preferred_element_type=F32)
    act = (gate / (1.0 + jnp.exp(-gate)) * up).astype(BF16)
    acc_ref[...] += jnp.dot(act, wd_ref[...], preferred_element_type=F32)

    @pl.when(f == nf - 1)
    def _():
        y = ALPHA * h_ref[...] + 0.5 * acc_ref[...]
        out = _layer_norm(y, g_ref[...], b_ref[...])
        o_ref[...] = out
        ob_ref[...] = out.astype(BF16)


def _ffn_ln(h, wg, wu, wd, g, b, *, tm, tf):
    n, d = h.shape
    ff = wg.shape[1]
    grid = (n // tm, ff // tf)
    return pl.pallas_call(
        _ffn_ln_kernel,
        out_shape=(jax.ShapeDtypeStruct((n, d), F32), jax.ShapeDtypeStruct((n, d), BF16)),
        grid=grid,
        in_specs=[
            pl.BlockSpec((tm, d), lambda i, f: (i, 0)),
            pl.BlockSpec((d, tf), lambda i, f: (0, f)),
            pl.BlockSpec((d, tf), lambda i, f: (0, f)),
            pl.BlockSpec((tf, d), lambda i, f: (f, 0)),
            pl.BlockSpec((1, d), lambda i, f: (0, 0)),
            pl.BlockSpec((1, d), lambda i, f: (0, 0)),
        ],
        out_specs=(
            pl.BlockSpec((tm, d), lambda i, f: (i, 0)),
            pl.BlockSpec((tm, d), lambda i, f: (i, 0)),
        ),
        scratch_shapes=[pltpu.VMEM((tm, d), BF16), pltpu.VMEM((tm, d), F32)],
        compiler_params=pltpu.CompilerParams(
            dimension_semantics=("arbitrary", "arbitrary"), vmem_limit_bytes=VMEM_LIMIT),
        name="ffn_ln",
    )(h, wg, wu, wd, g, b)


def _in_proj_kernel(x_ref, w_ref, q_ref, k_ref, v_ref, u_ref, *, tk):
    x = x_ref[...]
    tm = x.shape[0]
    aw = N_HEADS * V_DIM
    pair = 2 * V_DIM

    def heads(base, scale=None):
        for hp in range(N_HEADS // 2):
            res = jnp.dot(x, w_ref[:, base + hp * pair:base + (hp + 1) * pair], preferred_element_type=F32)
            if scale is not None:
                res = res * scale
            yield 2 * hp, res[:, 0:V_DIM]
            yield 2 * hp + 1, res[:, V_DIM:pair]

    for h, q in heads(0, LOG2E * QK_DIM ** -0.5):
        if tk:
            q_ref[h, 0] = q.T.astype(BF16)
        else:
            q_ref[h] = q.astype(BF16)
    for h, k in heads(aw):
        k_ref[h] = k.astype(BF16)
    for h, v in heads(2 * aw):
        if tk:
            for c in range(tm // tk):
                v_ref[h, c, 0:V_DIM, :] = v[c * tk:(c + 1) * tk, :].T.astype(BF16)
                v_ref[h, c, V_DIM:V_DIM + ONES_ROWS, :] = jnp.ones((ONES_ROWS, tk), BF16)
        else:
            v_ref[h] = v.astype(BF16)
    u_ref[...] = jnp.dot(x, w_ref[:, 3 * aw:], preferred_element_type=F32)


def _in_proj(x, w, *, tm, tk=None):
    n, d = x.shape
    aw = N_HEADS * V_DIM
    pw = w.shape[1] - 3 * aw
    row = lambda i: (i, 0)
    plain = (jax.ShapeDtypeStruct((N_HEADS, n, V_DIM), BF16), pl.BlockSpec((N_HEADS, tm, V_DIM), lambda i: (0, i, 0)))
    if tk:
        q_out = (jax.ShapeDtypeStruct((N_HEADS, n // tm, V_DIM, tm), BF16),
                 pl.BlockSpec((N_HEADS, 1, V_DIM, tm), lambda i: (0, i, 0, 0)))
        v_out = (jax.ShapeDtypeStruct((N_HEADS, n // tk, V_DIM + ONES_ROWS, tk), BF16),
                 pl.BlockSpec((N_HEADS, tm // tk, V_DIM + ONES_ROWS, tk), lambda i: (0, i, 0, 0)))
    else:
        q_out = v_out = plain
    u_out = (jax.ShapeDtypeStruct((n, pw), F32), pl.BlockSpec((tm, pw), row))
    outs = (q_out, plain, v_out, u_out)
    return pl.pallas_call(
        functools.partial(_in_proj_kernel, tk=tk),
        out_shape=tuple(o[0] for o in outs),
        grid=(n // tm,),
        in_specs=[
            pl.BlockSpec((tm, d), row),
            pl.BlockSpec(w.shape, lambda i: (0, 0), pipeline_mode=pl.Buffered(1)),
        ],
        out_specs=tuple(o[1] for o in outs),
        compiler_params=pltpu.CompilerParams(
            dimension_semantics=("arbitrary",), vmem_limit_bytes=VMEM_LIMIT),
        name="in_proj",
    )(x, w)


def _attn_kernel(q_ref, k_ref, vt_ref, km_ref, vm_ref, slope_ref, lq1_ref, lk1_ref, lq2_ref, lk2_ref,
                 g_ref, o_ref, qs_ref, m_ref, acc_ref, bias_ref,
                 s0_ref, s1_ref, c0_ref, c1_ref, p0_ref, p1_ref, a0_ref, a1_ref, *, tq, tk):
    qi = pl.program_id(2)
    cols = 2 * tq
    nd = tq // tk
    n_off = nd * qi
    slope = slope_ref[0] * LOG2E
    tn = (((0,), (0,)), ((), ()))
    s_refs, c_refs, p_refs, a_refs = (s0_ref, s1_ref), (c0_ref, c1_ref), (p0_ref, p1_ref), (a0_ref, a1_ref)

    def query_index(shape):
        c = lax.broadcasted_iota(jnp.int32, shape, 1)
        return jnp.where(c >= tq, c - tq, c)

    @pl.when(qi == 0)
    def _():
        i = query_index((tk, cols))
        j = lax.broadcasted_iota(jnp.int32, (tk, cols), 0)
        bias_ref[0] = -slope * (i - j).astype(F32)
        for t in range(nd):
            jt = j + t * tk
            visible = (i // CHUNK) >= (jt // CHUNK)
            bias_ref[1 + t] = jnp.where(visible, -slope * jnp.abs(i - jt).astype(F32), NEG_INF)

    qt = q_ref[0, 0].astype(F32)
    row = lax.broadcasted_iota(jnp.int32, qt.shape, 0)
    qs_ref[:, 0:tq] = jnp.where(row < QK_DIM, qt, 0.0).astype(BF16)
    qs_ref[:, tq:cols] = jnp.where(row >= QK_DIM, qt, 0.0).astype(BF16)

    s = jnp.dot(km_ref[0], qs_ref[...], preferred_element_type=F32)
    i = query_index((N_META, cols))
    j = lax.broadcasted_iota(jnp.int32, (N_META, cols), 0)
    s = s - slope * (i - j + (N_META + qi * tq)).astype(F32)
    m0 = jnp.max(s, axis=0, keepdims=True)
    p = jnp.exp2(s - m0)
    pb = p.astype(BF16)
    m_ref[...] = m0
    acc_ref[0:V_DIM, :] = lax.dot_general(vm_ref[0], pb, tn, preferred_element_type=F32)
    acc_ref[V_DIM:V_DIM + ONES_ROWS, :] = jnp.broadcast_to(
        jnp.sum(pb.astype(F32), axis=0, keepdims=True), (ONES_ROWS, cols))

    def scores(b, slot):
        start = pl.multiple_of(b * tk, tk)
        table = jnp.where(b < n_off, 0, b - n_off + 1)
        s = jnp.dot(k_ref[0, pl.ds(start, tk), :], qs_ref[...], preferred_element_type=F32)
        s = s + bias_ref[table]
        s_refs[slot][...] = s
        c_refs[slot][...] = jnp.max(s, axis=0, keepdims=True)

    def softmax(b, slot):
        gap = jnp.where(b < n_off, qi * tq - b * tk, 0)
        shift = -slope * jnp.full((1, cols), gap, jnp.int32).astype(F32)
        m_prev = m_ref[...]
        m_new = jnp.maximum(m_prev, c_refs[slot][...] + shift)
        offset = m_new - shift
        for c in range(0, cols, STRIP):
            p = jnp.exp2(s_refs[slot][:, c:c + STRIP] - offset[:, c:c + STRIP])
            p_refs[slot][:, c:c + STRIP] = p.astype(BF16)
        a_refs[slot][...] = jnp.exp2(m_prev - m_new)
        m_ref[...] = m_new

    def values(b, slot):
        pv = jnp.dot(vt_ref[0, b], p_refs[slot][...], preferred_element_type=F32)
        acc_ref[...] = a_refs[slot][...] * acc_ref[...] + pv

    scores(0, 0)
    scores(1, 1)
    softmax(0, 0)

    def pair(t, carry):
        b = 2 * t
        softmax(b - 1, 1)
        values(b - 2, 0)
        scores(b, 0)
        softmax(b, 0)
        values(b - 1, 1)
        scores(b + 1, 1)
        return carry

    n_blocks = n_off + nd
    lax.fori_loop(1, n_blocks // 2, pair, 0)
    softmax(n_blocks - 1, 1)
    values(n_blocks - 2, 0)
    values(n_blocks - 1, 1)

    lam = (jnp.exp(jnp.sum(lq1_ref[...] * lk1_ref[...], axis=1, keepdims=True))
           - jnp.exp(jnp.sum(lq2_ref[...] * lk2_ref[...], axis=1, keepdims=True)) + LAM_INIT)
    o = acc_ref[0:V_DIM, :] / acc_ref[V_DIM:V_DIM + 1, :]
    d = o[:, 0:tq] - lam * o[:, tq:cols]
    d = d * lax.rsqrt(jnp.mean(d * d, axis=0, keepdims=True) + LN_EPS)
    o_ref[0] = (d.T * g_ref[...] * (1.0 - LAM_INIT)).astype(o_ref.dtype)


def _diff_attn(q, k, vt, km, vm, slopes, lq1, lk1, lq2, lk2, g, *, b, tq, tk):
    s = k.shape[1] // b
    nq = s // tq
    assert q.shape[3] == tq and vt.shape[3] == tk
    cols = 2 * tq
    assert tq == 2 * tk, "the pipeline below walks key blocks in pairs"
    small = lambda shape: pl.BlockSpec(shape, lambda bi, h, i: (0,) * len(shape))
    slot = lambda shape, dtype: [pltpu.VMEM(shape, dtype), pltpu.VMEM(shape, dtype)]
    return pl.pallas_call(
        functools.partial(_attn_kernel, tq=tq, tk=tk),
        out_shape=jax.ShapeDtypeStruct((b, s, N_HEADS * V_DIM), BF16),
        grid=(b, N_HEADS, s // tq),
        in_specs=[
            pl.BlockSpec((1, 1, V_DIM, tq), lambda bi, h, i: (h, bi * nq + i, 0, 0)),
            pl.BlockSpec((1, s, V_DIM), lambda bi, h, i: (h, bi, 0)),
            pl.BlockSpec((1, s // tk, V_DIM + ONES_ROWS, tk), lambda bi, h, i: (h, bi, 0, 0)),
            pl.BlockSpec((1, N_META, V_DIM), lambda bi, h, i: (h, 0, 0)),
            pl.BlockSpec((1, N_META, V_DIM), lambda bi, h, i: (h, 0, 0)),
            pl.BlockSpec((1, 1, cols), lambda bi, h, i: (h, 0, 0)),
            small((1, QK_DIM)), small((1, QK_DIM)), small((1, QK_DIM)), small((1, QK_DIM)),
            small((1, V_DIM)),
        ],
        out_specs=pl.BlockSpec((1, tq, V_DIM), lambda bi, h, i: (bi, i, h)),
        scratch_shapes=[
            pltpu.VMEM((V_DIM, cols), BF16),
            pltpu.VMEM((1, cols), F32),
            pltpu.VMEM((V_DIM + ONES_ROWS, cols), F32),
            pltpu.VMEM((1 + tq // tk, tk, cols), F32),
            *slot((tk, cols), F32),
            *slot((1, cols), F32),
            *slot((tk, cols), BF16),
            *slot((1, cols), F32),
        ],
        compiler_params=pltpu.CompilerParams(
            dimension_semantics=("arbitrary", "arbitrary", "arbitrary"), vmem_limit_bytes=VMEM_LIMIT),
        name="diff_attn",
    )(q, k, vt, km, vm, slopes, lq1, lk1, lq2, lk2, g)


def _mix_ln_kernel(a_ref, u_ref, halo_ref, um_ref, h_ref, wp_ref, ps_ref, wo_ref, g_ref, b_ref,
                   o_ref, ext_ref, p_ref, *, tm):
    i = pl.program_id(1)
    hw = N_META

    @pl.when(i == 0)
    def _():
        ext_ref[0:hw, :] = um_ref[...]

    @pl.when(i > 0)
    def _():
        ext_ref[0:hw, :] = halo_ref[0]

    ext_ref[hw:hw + tm, :] = u_ref[0]

    for gi, w in enumerate(POOL_WINDOWS):
        cols = slice(gi * POOL_GROUP, (gi + 1) * POOL_GROUP)
        cur = ext_ref[hw:hw + tm, cols]
        win = cur
        for back in range(1, w):
            win = win + ext_ref[hw - back:hw - back + tm, cols]
        pooled = win * (1.0 / w) - cur
        y = jnp.dot(pooled.astype(BF16), wp_ref[gi], preferred_element_type=F32)
        p_ref[:, cols] = (y * ps_ref[:, cols]).astype(BF16)

    aw = a_ref.shape[2]
    mix = jnp.dot(a_ref[0], wo_ref[0:aw, :], preferred_element_type=F32)
    mix = mix + jnp.dot(p_ref[...], wo_ref[aw:, :], preferred_element_type=F32)
    o_ref[0] = _layer_norm(ALPHA * h_ref[0] + mix, g_ref[...], b_ref[...])


def _mix_ln(a, u, um, h, wp, ps, wo, g, b, *, tm):
    bsz, s, d = h.shape
    aw = a.shape[2]
    pw = u.shape[2]
    hb = tm // N_META
    const2 = lambda shape: pl.BlockSpec(shape, lambda bi, i: (0,) * len(shape), pipeline_mode=pl.Buffered(1))
    return pl.pallas_call(
        functools.partial(_mix_ln_kernel, tm=tm),
        out_shape=jax.ShapeDtypeStruct((bsz, s, d), F32),
        grid=(bsz, s // tm),
        in_specs=[
            pl.BlockSpec((1, tm, aw), lambda bi, i: (bi, i, 0)),
            pl.BlockSpec((1, tm, pw), lambda bi, i: (bi, i, 0)),
            pl.BlockSpec((1, N_META, pw), lambda bi, i: (bi, jnp.maximum(i * hb - 1, 0), 0)),
            const2((N_META, pw)),
            pl.BlockSpec((1, tm, d), lambda bi, i: (bi, i, 0)),
            const2(wp.shape),
            const2((1, pw)),
            const2(wo.shape),
            const2((1, d)),
            const2((1, d)),
        ],
        out_specs=pl.BlockSpec((1, tm, d), lambda bi, i: (bi, i, 0)),
        scratch_shapes=[pltpu.VMEM((tm + N_META, pw), F32), pltpu.VMEM((tm, pw), BF16)],
        compiler_params=pltpu.CompilerParams(
            dimension_semantics=("arbitrary", "arbitrary"), vmem_limit_bytes=VMEM_LIMIT),
        name="mix_ln",
    )(a, u, u, um, h, wp, ps, wo, g, b)


def kernel(x, meta_tokens, ln1_g, ln1_b, ffn1_w_gate, ffn1_w_up, ffn1_w_down, w_in, lambda_q1, lambda_k1,
           lambda_q2, lambda_k2, subln_g, w_pool, pool_scale, w_out, ln2_g, ln2_b, ffn2_w_gate, ffn2_w_up,
           ffn2_w_down, ln3_g, ln3_b):
    bsz, s, d = x.shape
    l = 0
    tm, tf, tq, tk = 512, 512, 512, 256

    wg1, wu1, wd1 = (w[l].astype(BF16) for w in (ffn1_w_gate, ffn1_w_up, ffn1_w_down))
    wg2, wu2, wd2 = (w[l].astype(BF16) for w in (ffn2_w_gate, ffn2_w_up, ffn2_w_down))
    w_in_b = w_in[l].astype(BF16)
    w_pool_b = w_pool[l].astype(BF16)
    w_out_b = w_out[l].astype(BF16)
    row = lambda p: p[l].reshape(1, -1).astype(F32)

    h1, h1b = _ffn_ln(x.reshape(bsz * s, d), wg1, wu1, wd1, row(ln1_g), row(ln1_b), tm=tm, tf=tf)
    _, m1b = _ffn_ln(meta_tokens.astype(F32), wg1, wu1, wd1, row(ln1_g), row(ln1_b), tm=N_META, tf=tf)
    q, k, vt, u = _in_proj(h1b, w_in_b, tm=tq, tk=tk)
    _, km, vm, um = _in_proj(m1b, w_in_b, tm=N_META)

    slopes = 2.0 ** (-8.0 * jnp.arange(1, N_HEADS + 1, dtype=F32) / N_HEADS)
    slopes = jnp.broadcast_to(slopes[:, None, None], (N_HEADS, 1, 2 * tq))
    a_out = _diff_attn(q, k, vt, km, vm, slopes, row(lambda_q1), row(lambda_k1), row(lambda_q2), row(lambda_k2),
                       row(subln_g), b=bsz, tq=tq, tk=tk)

    h2 = _mix_ln(a_out, u.reshape(bsz, s, -1), um, h1.reshape(bsz, s, d), w_pool_b, row(pool_scale), w_out_b,
                 row(ln2_g), row(ln2_b), tm=tm)

    out, _ = _ffn_ln(h2.reshape(bsz * s, d), wg2, wu2, wd2, row(ln3_g), row(ln3_b), tm=tm, tf=tf)
    return out.reshape(bsz, s, d)
```
